```python
import jax
import jax.numpy as jnp
from jax import lax
import numpy as np

D_MODEL = 1024
BATCH = 32
SEQ = 2048
DEPTH = 4

CTX_LEN = 256
GRID_W = 64

ATT_HEADS = 8
ATT_KV_HEADS = 2
ATT_HEAD_DIM = 64
ATT_WINDOW = 128
ATT_BLOCK = 128
ROPE_BASE = 10000.0

HG_HEADS = 4
HG_KEY_DIM = 128
HG_VAL_DIM = 128
HG_CHUNK = 32

ATT_Q_W = ATT_HEADS * ATT_HEAD_DIM
ATT_KV_W = ATT_KV_HEADS * ATT_HEAD_DIM
HG_K_W = HG_HEADS * HG_KEY_DIM
HG_V_W = HG_HEADS * HG_VAL_DIM
MIX_FIELDS = (("att_q", ATT_Q_W), ("att_k", ATT_KV_W), ("att_v", ATT_KV_W), ("hg_q", HG_K_W), ("hg_f_fwd", HG_K_W), ("hg_f_bwd", HG_K_W), ("hg_i", HG_V_W), ("hg_g", HG_V_W))
ALL_FIELDS = ("att_q", "att_k", "att_v", "hg_q", "hg_f_fwd", "hg_f_bwd", "hg_i", "hg_g")
CTX_STATE_FIELDS = ("att_k", "att_v", "hg_f_fwd", "hg_f_bwd", "hg_i")
MIX_IN_W = ATT_Q_W + 2 * ATT_KV_W + 3 * HG_K_W + 2 * HG_V_W
MIX_OUT_W = ATT_Q_W + HG_V_W

CONF_KERNEL = 31
FFN_HIDDEN = 2816
FFN_KERNEL = 3

N_MIX_LAYERS = (DEPTH + 1) // 2
N_CONV_LAYERS = DEPTH // 2
DEEPNORM_ALPHA = (2 * DEPTH) ** 0.25
DEEPNORM_BETA = (8 * DEPTH) ** -0.25
LN_EPS = 1e-5
RMS_EPS = 1e-6
MASK_VALUE = -1e30
LB_FLOOR = 1e-30

kernel_name = "hybrid_swa_hgrn2_conformer_dit"


def layer_norm(x, g, b):
    xf = x.astype(jnp.float32)
    mu = jnp.mean(xf, axis=-1, keepdims=True)
    var = jnp.mean(jnp.square(xf - mu), axis=-1, keepdims=True)
    return ((xf - mu) * lax.rsqrt(var + LN_EPS) * g + b).astype(x.dtype)


def modulation(cond, w, b):
    return jnp.split(jnp.dot(jax.nn.silu(cond), w) + b, 6, axis=-1)


def modulate(h, shift, scale):
    return h * (1 + scale) + shift


def post_norm(res, gate, y, g, b):
    return layer_norm(DEEPNORM_ALPHA * res + gate * y, g, b)


def dwconv(x, w, b):
    k = w.shape[0]
    y = lax.conv_general_dilated(x, w[:, None, :], window_strides=(1,), padding=[((k - 1) // 2, (k - 1) // 2)], dimension_numbers=("NWC", "WIO", "NWC"), feature_group_count=x.shape[-1])
    return y + b


def split_heads(a, n_heads):
    B, T, W = a.shape
    return a.reshape(B, T, n_heads, W // n_heads).transpose(0, 2, 1, 3)


def merge_heads(a):
    B, H, T, d = a.shape
    return a.transpose(0, 2, 1, 3).reshape(B, T, H * d)


def combined_projection(h, w_in, names):
    bounds, off = {}, 0
    for name, width in MIX_FIELDS:
        bounds[name] = (off, off + width)
        off += width
    if len(names) == len(MIX_FIELDS):
        y = h @ w_in
        return {n: y[..., lo:hi] for n, (lo, hi) in bounds.items()}
    return {n: h @ w_in[:, bounds[n][0]:bounds[n][1]] for n in names}


def axial_rope_tables(rows):
    row = jnp.repeat(jnp.arange(rows), GRID_W).astype(jnp.float32)
    col = jnp.tile(jnp.arange(GRID_W), rows).astype(jnp.float32)
    quarter = ATT_HEAD_DIM // 4
    inv_freq = ROPE_BASE ** (-jnp.arange(quarter, dtype=jnp.float32) / quarter)
    ang_r = row[:, None] * inv_freq
    ang_c = col[:, None] * inv_freq
    ang = jnp.concatenate([ang_r, ang_r, ang_c, ang_c], axis=-1)
    return jnp.cos(ang), jnp.sin(ang)


def apply_axial_rope(x, cos, sin):
    q4 = x.shape[-1] // 4
    xr = x.reshape(x.shape[:-1] + (2, 2, q4))
    rot = jnp.stack([-xr[..., 1, :], xr[..., 0, :]], axis=-2).reshape(x.shape)
    return (x * cos + rot * sin).astype(x.dtype)


def banded_attention(q, k, v, kc, vc, sink):
    B, Hkv, G, T, dh = q.shape
    L = kc.shape[2]
    nb = T // ATT_BLOCK
    span = ATT_BLOCK + 2 * ATT_WINDOW
    pad = ((0, 0), (0, 0), (ATT_WINDOW, ATT_WINDOW), (0, 0))
    kp = jnp.pad(k, pad)
    vp = jnp.pad(v, pad)
    qb = jnp.moveaxis(q.reshape(B, Hkv, G, nb, ATT_BLOCK, dh), 3, 0)
    sink_col = jnp.broadcast_to(sink.astype(jnp.float32)[None, :, :, None, None], (B, Hkv, G, ATT_BLOCK, 1))
    offs = jnp.arange(span) - ATT_WINDOW
    in_window = jnp.abs(offs[None, :] - jnp.arange(ATT_BLOCK)[:, None]) <= ATT_WINDOW

    def one_block(args):
        i, qi = args
        start = i * ATT_BLOCK
        ki = lax.dynamic_slice_in_dim(kp, start, span, axis=2)
        vi = lax.dynamic_slice_in_dim(vp, start, span, axis=2)
        kpos = start + offs
        valid = in_window & ((kpos >= 0) & (kpos < T))[None, :]
        s_loc = jnp.where(valid, jnp.einsum("bhgqd,bhkd->bhgqk", qi, ki).astype(jnp.float32), MASK_VALUE)
        s_ctx = jnp.einsum("bhgqd,bhcd->bhgqc", qi, kc).astype(jnp.float32)
        p = jax.nn.softmax(jnp.concatenate([s_loc, s_ctx, sink_col], axis=-1), axis=-1).astype(v.dtype)
        return jnp.einsum("bhgqk,bhkd->bhgqd", p[..., :span], vi) + jnp.einsum("bhgqc,bhcd->bhgqd", p[..., span:span + L], vc)

    o = lax.map(one_block, (jnp.arange(nb), qb))
    return jnp.moveaxis(o, 0, 3).reshape(B, Hkv, G, T, dh)


def context_attention(q, k, v, sink):
    B, Hkv, G, L, dh = q.shape
    s = jnp.einsum("bhgqd,bhkd->bhgqk", q, k).astype(jnp.float32)
    sink_col = jnp.broadcast_to(sink.astype(jnp.float32)[None, :, :, None, None], (B, Hkv, G, L, 1))
    p = jax.nn.softmax(jnp.concatenate([s, sink_col], axis=-1), axis=-1)[..., :L].astype(v.dtype)
    return jnp.einsum("bhgqk,bhkd->bhgqd", p, v)


def hgrn_lower_bounds(logits):
    p = jax.nn.softmax(logits.astype(jnp.float32), axis=0)
    return jnp.cumsum(p, axis=0) - p[0]


def hgrn2_gates(f_raw, lb):
    f_raw = f_raw.astype(jnp.float32)
    log_lb = jnp.log(jnp.maximum(lb, LB_FLOOR))
    log_f = jnp.logaddexp(log_lb, jnp.log1p(-lb) + jax.nn.log_sigmoid(f_raw))
    return log_f, (1 - lb) * jax.nn.sigmoid(-f_raw)


def gla_scan(q, k, v, log_f, s0):
    B, H, T, dk = q.shape
    dv = v.shape[-1]
    n = T // HG_CHUNK

    def to_chunks(a):
        return jnp.moveaxis(a.reshape(B, H, n, HG_CHUNK, a.shape[-1]), 2, 0)

    causal = jnp.tril(jnp.ones((HG_CHUNK, HG_CHUNK), dtype=bool))[:, :, None]

    def step(state, chunk):
        qc, kc, vc, gc = chunk
        b = jnp.cumsum(gc, axis=2)
        diff = b[:, :, :, None, :] - b[:, :, None, :, :]
        decay = jnp.exp(jnp.where(causal, diff, MASK_VALUE))
        scores = jnp.einsum("bhtk,bhsk,bhtsk->bhts", qc, kc, decay)
        out = jnp.einsum("bhts,bhsv->bhtv", scores, vc) + jnp.einsum("bhtk,bhkv->bhtv", qc * jnp.exp(b), state)
        b_end = b[:, :, -1:, :]
        state = jnp.exp(b_end)[:, :, 0, :, None] * state + jnp.einsum("bhsk,bhsv->bhkv", kc * jnp.exp(b_end - b), vc)
        return state, out

    s_fin, out = lax.scan(step, s0, (to_chunks(q), to_chunks(k), to_chunks(v), to_chunks(log_f)))
    return jnp.moveaxis(out, 0, 2).reshape(B, H, T, dv), s_fin


def gla_final_state(k, v, log_f):
    tail = lax.cumsum(log_f, axis=2, reverse=True) - log_f
    return jnp.einsum("bhsk,bhsv->bhkv", k * jnp.exp(tail), v)


def hgrn_readout(o, g_raw, norm_g):
    o = o * lax.rsqrt(jnp.mean(o * o, axis=-1, keepdims=True) + RMS_EPS) * norm_g
    return merge_heads(o).astype(g_raw.dtype) * jax.nn.silu(g_raw)


def flip_t(a):
    return jnp.flip(a, axis=2)


def attn_hgrn_mixer(hc, hx, need_ctx_out, w_in, w_out, sink, lb, norm_g, cos, sin):
    B, T, _ = hx.shape
    L = hc.shape[1]
    G = ATT_HEADS // ATT_KV_HEADS
    dh = ATT_HEAD_DIM
    scale = dh ** -0.5
    sink = sink.reshape(ATT_KV_HEADS, G)
    lb = lb.reshape(HG_HEADS, 1, HG_KEY_DIM)
    px = combined_projection(hx, w_in, ALL_FIELDS)
    pc = combined_projection(hc, w_in, ALL_FIELDS if need_ctx_out else CTX_STATE_FIELDS)

    qx = apply_axial_rope(split_heads(px["att_q"], ATT_HEADS), cos, sin) * scale
    kx = apply_axial_rope(split_heads(px["att_k"], ATT_KV_HEADS), cos, sin)
    vx = split_heads(px["att_v"], ATT_KV_HEADS)
    kc = split_heads(pc["att_k"], ATT_KV_HEADS)
    vc = split_heads(pc["att_v"], ATT_KV_HEADS)
    att_x = banded_attention(qx.reshape(B, ATT_KV_HEADS, G, T, dh), kx, vx, kc, vc, sink)
    att_x = merge_heads(att_x.reshape(B, ATT_HEADS, T, dh))

    qx_h = split_heads(px["hg_q"], HG_HEADS).astype(jnp.float32)
    vx_h = split_heads(px["hg_i"], HG_HEADS).astype(jnp.float32)
    lfx_f, kx_f = hgrn2_gates(split_heads(px["hg_f_fwd"], HG_HEADS), lb)
    lfx_b, kx_b = hgrn2_gates(split_heads(px["hg_f_bwd"], HG_HEADS), lb)
    vc_h = split_heads(pc["hg_i"], HG_HEADS).astype(jnp.float32)
    lfc_f, kc_f = hgrn2_gates(split_heads(pc["hg_f_fwd"], HG_HEADS), lb)
    lfc_b, kc_b = hgrn2_gates(split_heads(pc["hg_f_bwd"], HG_HEADS), lb)
    if need_ctx_out:
        qc_h = split_heads(pc["hg_q"], HG_HEADS).astype(jnp.float32)
        zero = jnp.zeros((B, HG_HEADS, HG_KEY_DIM, HG_VAL_DIM), jnp.float32)
        oc_f, sc_f = gla_scan(qc_h, kc_f, vc_h, lfc_f, zero)
        oc_b, sc_b = gla_scan(flip_t(qc_h), flip_t(kc_b), flip_t(vc_h), flip_t(lfc_b), zero)
        oc_b = flip_t(oc_b)
    else:
        sc_f = gla_final_state(kc_f, vc_h, lfc_f)
        sc_b = gla_final_state(flip_t(kc_b), flip_t(vc_h), flip_t(lfc_b))
    ox_f, _ = gla_scan(qx_h, kx_f, vx_h, lfx_f, sc_f)
    ox_b, _ = gla_scan(flip_t(qx_h), flip_t(kx_b), flip_t(vx_h), flip_t(lfx_b), sc_b)
    hg_x = hgrn_readout(ox_f + flip_t(ox_b), px["hg_g"], norm_g)

    out_x = jnp.concatenate([att_x, hg_x], axis=-1) @ w_out
    out_c = None
    if need_ctx_out:
        qcc = (split_heads(pc["att_q"], ATT_HEADS) * scale).reshape(B, ATT_KV_HEADS, G, L, dh)
        att_c = merge_heads(context_attention(qcc, kc, vc, sink).reshape(B, ATT_HEADS, L, dh))
        hg_c = hgrn_readout(oc_f + oc_b, pc["hg_g"], norm_g)
        out_c = jnp.concatenate([att_c, hg_c], axis=-1) @ w_out
    return out_x, out_c


def conformer_conv(h, pw1_w, pw1_b, dw_w, dw_b, ln_g, ln_b, pw2_w, pw2_b):
    a = h @ pw1_w + pw1_b
    a = a[..., :D_MODEL] * jax.nn.sigmoid(a[..., D_MODEL:])
    a = jax.nn.silu(layer_norm(dwconv(a, dw_w, dw_b), ln_g, ln_b))
    return a @ pw2_w + pw2_b


def conv_glu_ffn(h, w_up, dw_w, dw_b, w_down):
    u = h @ w_up
    gate = dwconv(u[..., :FFN_HIDDEN], dw_w, dw_b)
    return (jax.nn.silu(gate) * u[..., FFN_HIDDEN:]) @ w_down


def setup_inputs(seed: int = 0) -> dict:
    key = jax.random.key(seed)
    ks = iter(jax.random.split(key, 25))
    D = D_MODEL
    NM = N_MIX_LAYERS
    NC = N_CONV_LAYERS

    def nrm(shape, scale):
        return jax.random.normal(next(ks), shape, jnp.float32) * scale

    return {
        "x": nrm((BATCH, SEQ, D), 1.0),
        "c": nrm((BATCH, D), 1.0),
        "ctx": nrm((BATCH, CTX_LEN, D), 1.0),
        "c_ctx": nrm((D,), 1.0),
        "mod_w": nrm((DEPTH, D, 6 * D), D ** -0.5),
        "mod_b": nrm((DEPTH, 6 * D), 0.02),
        "post_ln_g": 1.0 + nrm((DEPTH, 2, D), 0.02),
        "post_ln_b": nrm((DEPTH, 2, D), 0.02),
        "mix_w_in": nrm((NM, D, MIX_IN_W), D ** -0.5),
        "mix_w_out": nrm((NM, MIX_OUT_W, D), DEEPNORM_BETA * MIX_OUT_W ** -0.5),
        "att_sink": nrm((NM, ATT_HEADS), 0.5),
        "hg_lb_logits": nrm((NM, HG_K_W), 0.5),
        "hg_norm_g": 1.0 + nrm((NM, HG_VAL_DIM), 0.02),
        "conf_pw1_w": nrm((NC, D, 2 * D), D ** -0.5),
        "conf_pw1_b": nrm((NC, 2 * D), 0.02),
        "conf_dw_w": nrm((NC, CONF_KERNEL, D), CONF_KERNEL ** -0.5),
        "conf_dw_b": nrm((NC, D), 0.02),
        "conf_ln_g": 1.0 + nrm((NC, D), 0.02),
        "conf_ln_b": nrm((NC, D), 0.02),
        "conf_pw2_w": nrm((NC, D, D), DEEPNORM_BETA * D ** -0.5),
        "conf_pw2_b": nrm((NC, D), 0.02),
        "ffn_w_up": nrm((DEPTH, D, 2 * FFN_HIDDEN), D ** -0.5),
        "ffn_dw_w": nrm((DEPTH, FFN_KERNEL, FFN_HIDDEN), FFN_KERNEL ** -0.5),
        "ffn_dw_b": nrm((DEPTH, FFN_HIDDEN), 0.02),
        "ffn_w_down": nrm((DEPTH, FFN_HIDDEN, D), DEEPNORM_BETA * FFN_HIDDEN ** -0.5),
    }


def reference(x, c, ctx, c_ctx, mod_w, mod_b, post_ln_g, post_ln_b, mix_w_in, mix_w_out, att_sink, hg_lb_logits, hg_norm_g, conf_pw1_w, conf_pw1_b, conf_dw_w, conf_dw_b, conf_ln_g, conf_ln_b, conf_pw2_w, conf_pw2_b, ffn_w_up, ffn_dw_w, ffn_dw_b, ffn_w_down):
    T = x.shape[1]
    rows = T // GRID_W
    cos, sin = axial_rope_tables(rows)
    lower_bounds = hgrn_lower_bounds(hg_lb_logits)
    h_ctx = ctx
    for layer in range(DEPTH):
        m = layer // 2
        is_mix = layer % 2 == 0
        need_ctx_out = any(j % 2 == 0 for j in range(layer + 1, DEPTH))
        mx = [t[:, None, :] for t in modulation(c, mod_w[layer], mod_b[layer])]
        hx = modulate(x, mx[0], mx[1])
        if is_mix or need_ctx_out:
            mc = modulation(c_ctx, mod_w[layer], mod_b[layer])
            hc = modulate(h_ctx, mc[0], mc[1])
        if is_mix:
            y_x, y_c = attn_hgrn_mixer(hc, hx, need_ctx_out, mix_w_in[m], mix_w_out[m], att_sink[m], lower_bounds[m], hg_norm_g[m], cos, sin)
        else:
            conf = (conf_pw1_w[m], conf_pw1_b[m], conf_dw_w[m], conf_dw_b[m], conf_ln_g[m], conf_ln_b[m], conf_pw2_w[m], conf_pw2_b[m])
            y_x = conformer_conv(hx, *conf)
            y_c = conformer_conv(hc, *conf) if need_ctx_out else None
        ffn = (ffn_w_up[layer], ffn_dw_w[layer], ffn_dw_b[layer], ffn_w_down[layer])
        x = post_norm(x, mx[2], y_x, post_ln_g[layer, 0], post_ln_b[layer, 0])
        x = post_norm(x, mx[5], conv_glu_ffn(modulate(x, mx[3], mx[4]), *ffn), post_ln_g[layer, 1], post_ln_b[layer, 1])
        if need_ctx_out:
            h_ctx = post_norm(h_ctx, mc[2], y_c, post_ln_g[layer, 0], post_ln_b[layer, 0])
            h_ctx = post_norm(h_ctx, mc[5], conv_glu_ffn(modulate(h_ctx, mc[3], mc[4]), *ffn), post_ln_g[layer, 1], post_ln_b[layer, 1])
        else:
            h_ctx = None
    return x
```

```python
import functools

import jax
import jax.numpy as jnp
from jax import lax
from jax.experimental import pallas as pl
from jax.experimental.pallas import tpu as pltpu

F32 = jnp.float32
BF16 = jnp.bfloat16

D_MODEL = 1024
DEPTH = 4
GRID_W = 64
ATT_HEADS = 8
ATT_KV_HEADS = 2
ATT_HEAD_DIM = 64
ATT_WINDOW = 128
ATT_BLOCK = 128
ROPE_BASE = 10000.0
HG_HEADS = 4
HG_DIM = 128
CONF_KERNEL = 31
FFN_HIDDEN = 2816
DEEPNORM_ALPHA = (2 * DEPTH) ** 0.25
LN_EPS = 1e-5
RMS_EPS = 1e-6
MASK_VALUE = -1e30
LB_FLOOR = 1e-30

ATT_Q_W = ATT_HEADS * ATT_HEAD_DIM
ATT_KV_W = ATT_KV_HEADS * ATT_HEAD_DIM
HG_W = HG_HEADS * HG_DIM
OFF_AQ = 0
OFF_AK = OFF_AQ + ATT_Q_W
OFF_AV = OFF_AK + ATT_KV_W
OFF_HQ = OFF_AV + ATT_KV_W
OFF_FF = OFF_HQ + HG_W
OFF_FB = OFF_FF + HG_W
OFF_HI = OFF_FB + HG_W
OFF_HGATE = OFF_HI + HG_W
MIX_IN_W = OFF_HGATE + HG_W

LANES = 128
SUBLANES = 8
TM = 256
CHUNK = 128
CHUNK_LEVELS = 7
FFN_HALO = SUBLANES
CONF_HALO = 16
VMEM_LIMIT = 56 * 1024 * 1024


def _cparams(n_grid):
    return pltpu.CompilerParams(
        dimension_semantics=("arbitrary",) * n_grid, vmem_limit_bytes=VMEM_LIMIT)


def _dot(a, b):
    return jnp.dot(a, b, preferred_element_type=F32)


def _dot_nt(a, b):
    return lax.dot_general(a, b, (((1,), (1,)), ((), ())), preferred_element_type=F32)


def _dot_tn(a, b):
    return lax.dot_general(a, b, (((0,), (0,)), ((), ())), preferred_element_type=F32)


def _sigmoid(x):
    e = jnp.exp(-jnp.abs(x))
    r = 1.0 / (1.0 + e)
    return jnp.where(x >= 0, r, e * r)


def _silu(x):
    return x * _sigmoid(x)


def _layer_norm(z, g, b):
    mu = jnp.mean(z, axis=-1, keepdims=True)
    zc = z - mu
    var = jnp.mean(zc * zc, axis=-1, keepdims=True)
    return zc * lax.rsqrt(var + LN_EPS) * g + b


def _resident(shape):
    nd = len(shape)
    return pl.BlockSpec(shape, lambda *_: (0,) * nd, pipeline_mode=pl.Buffered(1))


MOD_TN = 1536


def _mod_kernel(cond_ref, w_ref, b_ref, o_ref):
    s = _silu(cond_ref[...])
    hi = s.astype(BF16)
    lo = (s - hi.astype(F32)).astype(BF16)
    w = w_ref[0]
    o_ref[0] = _dot(hi, w) + _dot(lo, w) + b_ref[0]


def _modulation(cond, mod_w, mod_b):
    R = cond.shape[0]
    n6 = mod_w.shape[2]
    return pl.pallas_call(
        _mod_kernel,
        grid=(DEPTH, n6 // MOD_TN),
        in_specs=[
            pl.BlockSpec((R, D_MODEL), lambda l, j: (0, 0)),
            pl.BlockSpec((1, D_MODEL, MOD_TN), lambda l, j: (l, 0, j)),
            pl.BlockSpec((1, 1, MOD_TN), lambda l, j: (l, 0, j)),
        ],
        out_specs=pl.BlockSpec((1, R, MOD_TN), lambda l, j: (l, 0, j)),
        out_shape=jax.ShapeDtypeStruct((DEPTH, R, n6), F32),
        compiler_params=_cparams(2),
        name="modulation",
    )(cond, mod_w, mod_b)


def _mix_in_kernel(x_ref, mod_ref, w_ref, cos_ref, sa_ref, sb_ref, lbl_ref,
                   q_ref, k_ref, v_ref, hq_ref, lff_ref, kf_ref, lfb_ref, kb_ref, hi_ref, hg_ref,
                   *, layer_m):
    x = x_ref[0]
    shift = mod_ref[0, 0, 0:1, :]
    scale = mod_ref[0, 0, 1:2, :]
    h = (x * (1.0 + scale) + shift).astype(BF16)

    def proj(off, width):
        return _dot(h, w_ref[:, off:off + width])

    cos = cos_ref[...]
    sa = sa_ref[...]
    sb = sb_ref[...]

    def rope(a, reps):
        w = a.shape[1]
        c = jnp.concatenate([cos] * reps, axis=1) if reps > 1 else cos
        s1 = jnp.concatenate([sa] * reps, axis=1) if reps > 1 else sa
        s2 = jnp.concatenate([sb] * reps, axis=1) if reps > 1 else sb
        up = pltpu.roll(a, w - 16, axis=1)
        dn = pltpu.roll(a, 16, axis=1)
        return a * c + up * s1 + dn * s2

    q = rope(proj(OFF_AQ, ATT_Q_W), ATT_Q_W // LANES) * (ATT_HEAD_DIM ** -0.5)
    q_ref[0] = q.astype(BF16)

    lane = lax.broadcasted_iota(jnp.int32, (TM, LANES), 1)
    first = lane < ATT_HEAD_DIM

    def pair_rep(a):
        sw = pltpu.roll(a, ATT_HEAD_DIM, axis=1)
        return jnp.concatenate([jnp.where(first, a, sw), jnp.where(first, sw, a)], axis=1)

    k_ref[0] = pair_rep(rope(proj(OFF_AK, ATT_KV_W), 1)).astype(BF16)
    v_ref[0] = pair_rep(proj(OFF_AV, ATT_KV_W)).astype(BF16)

    hq_ref[0] = proj(OFF_HQ, HG_W)
    hi_ref[0] = proj(OFF_HI, HG_W).astype(BF16)
    hg_ref[0] = proj(OFF_HGATE, HG_W)

    logits = lbl_ref[...]
    n_mix = logits.shape[0]
    mx = logits[0:1, :]
    for r in range(1, n_mix):
        mx = jnp.maximum(mx, logits[r:r + 1, :])
    ex = [jnp.exp(logits[r:r + 1, :] - mx) for r in range(n_mix)]
    tot = ex[0]
    for r in range(1, n_mix):
        tot = tot + ex[r]
    cum = ex[0] / tot
    p0 = cum
    for r in range(1, layer_m + 1):
        cum = cum + ex[r] / tot
    lb = cum - p0
    log_lb = jnp.log(jnp.maximum(lb, LB_FLOOR))
    log_1m = jnp.log1p(-lb)
    one_m = 1.0 - lb

    def gates(fr, lf_out, k_out):
        e = jnp.exp(-jnp.abs(fr))
        l1pe = jnp.log1p(e)
        c = log_1m + (jnp.minimum(fr, 0.0) - l1pe)
        lf = jnp.maximum(log_lb, c) + jnp.log1p(jnp.exp(-jnp.abs(log_lb - c)))
        r = 1.0 / (1.0 + e)
        sig_neg = jnp.where(fr >= 0, e * r, r)
        lf_out[0] = lf
        k_out[0] = one_m * sig_neg

    gates(proj(OFF_FF, HG_W), lff_ref, kf_ref)
    gates(proj(OFF_FB, HG_W), lfb_ref, kb_ref)


def _mix_in(xs, mods, w_in, cos, sa, sb, lb_logits, layer_m, n_ctx_tiles):
    B, S, _ = xs.shape
    nt = S // TM
    row = lambda w: pl.BlockSpec((1, TM, w), lambda b, i: (b, i, 0))
    tab = pl.BlockSpec((TM, LANES), lambda b, i: (i, 0))
    sds = lambda w, dt: jax.ShapeDtypeStruct((B, S, w), dt)
    return pl.pallas_call(
        functools.partial(_mix_in_kernel, layer_m=layer_m),
        grid=(B, nt),
        in_specs=[
            row(D_MODEL),
            pl.BlockSpec((1, 1, 6, D_MODEL), lambda b, i: (b, jnp.where(i < n_ctx_tiles, 0, 1), 0, 0)),
            _resident((D_MODEL, MIX_IN_W)),
            tab, tab, tab,
            _resident(lb_logits.shape),
        ],
        out_specs=[row(ATT_Q_W), row(2 * LANES), row(2 * LANES), row(HG_W), row(HG_W), row(HG_W),
                   row(HG_W), row(HG_W), row(HG_W), row(HG_W)],
        out_shape=[sds(ATT_Q_W, BF16), sds(2 * LANES, BF16), sds(2 * LANES, BF16), sds(HG_W, F32),
                   sds(HG_W, F32), sds(HG_W, F32), sds(HG_W, F32), sds(HG_W, F32), sds(HG_W, BF16),
                   sds(HG_W, F32)],
        compiler_params=_cparams(2),
        name="mix_in",
    )(xs, mods, w_in, cos, sa, sb, lb_logits)


GROUP = ATT_HEADS // ATT_KV_HEADS


def _attn_kernel(sink_ref, q_ref, k_ref, v_ref, o_ref, *, n_ctx, n_lat, ctx_out):
    hkv = pl.program_id(1)
    QB = ATT_BLOCK
    nb = n_lat // QB
    lane = lax.broadcasted_iota(jnp.int32, (QB, LANES), 1)
    first = lane < ATT_HEAD_DIM
    rows4 = lax.broadcasted_iota(jnp.int32, (GROUP * QB, LANES), 0)
    rq = rows4 & (QB - 1)
    col = lax.broadcasted_iota(jnp.int32, (GROUP * QB, LANES), 1)
    grp = lax.broadcasted_iota(jnp.int32, (GROUP * QB, 1), 0) // QB
    sink = jnp.zeros((GROUP * QB, 1), F32)
    for g in range(GROUP):
        sink = jnp.where(grp == g, sink_ref[hkv * GROUP + g], sink)

    def stack_q(q):
        qa, qb = q[:, :LANES], q[:, LANES:]
        z = jnp.zeros_like(qa)
        return jnp.concatenate([jnp.where(first, qa, z), jnp.where(first, z, qa),
                                jnp.where(first, qb, z), jnp.where(first, z, qb)], axis=0)

    def unstack_o(o):
        return jnp.concatenate([jnp.where(first, o[0:QB], o[QB:2 * QB]),
                                jnp.where(first, o[2 * QB:3 * QB], o[3 * QB:4 * QB])], axis=1)

    kc = k_ref[0, 0:n_ctx, :]
    vc = v_ref[0, 0:n_ctx, :]

    def softmax_pv(parts):
        m = sink
        for s, _ in parts:
            m = jnp.maximum(m, jnp.max(s, axis=-1, keepdims=True))
        den = jnp.exp(sink - m)
        acc = jnp.zeros((GROUP * QB, LANES), F32)
        for s, v in parts:
            p = jnp.exp(s - m)
            den = den + jnp.sum(p, axis=-1, keepdims=True)
            acc = acc + _dot(p.astype(BF16), v)
        return acc * (1.0 / den)

    def lat_block(i, carry):
        r0 = pl.multiple_of(n_ctx + i * QB, QB)
        rp = pl.multiple_of(r0 - QB, QB)
        rn = pl.multiple_of(jnp.minimum(r0 + QB, n_ctx + n_lat - QB), QB)
        q4 = stack_q(q_ref[0, pl.ds(r0, QB), :])
        lo_col = jnp.where(i > 0, rq, LANES)
        hi_col = jnp.where(i < nb - 1, rq, -1)
        sp = jnp.where(col >= lo_col, _dot_nt(q4, k_ref[0, pl.ds(rp, QB), :]), MASK_VALUE)
        ss = _dot_nt(q4, k_ref[0, pl.ds(r0, QB), :])
        sn = jnp.where(col <= hi_col, _dot_nt(q4, k_ref[0, pl.ds(rn, QB), :]), MASK_VALUE)
        sc = _dot_nt(q4, kc)
        o = softmax_pv([(sp, v_ref[0, pl.ds(rp, QB), :]), (ss, v_ref[0, pl.ds(r0, QB), :]),
                        (sn, v_ref[0, pl.ds(rn, QB), :]), (sc, vc)])
        o_ref[0, pl.ds(r0, QB), :] = unstack_o(o).astype(BF16)
        return carry

    lax.fori_loop(0, nb, lat_block, 0)

    for j in range(n_ctx // QB):
        if ctx_out:
            q4 = stack_q(q_ref[0, j * QB:(j + 1) * QB, :])
            o = softmax_pv([(_dot_nt(q4, kc), vc)])
            o_ref[0, j * QB:(j + 1) * QB, :] = unstack_o(o).astype(BF16)
        else:
            o_ref[0, j * QB:(j + 1) * QB, :] = jnp.zeros((QB, 2 * LANES), BF16)


def _attention(sink, q, k, v, n_ctx, ctx_out):
    B, S, _ = q.shape
    return pl.pallas_call(
        functools.partial(_attn_kernel, n_ctx=n_ctx, n_lat=S - n_ctx, ctx_out=ctx_out),
        grid=(B, ATT_KV_HEADS),
        in_specs=[
            pl.BlockSpec(memory_space=pltpu.SMEM),
            pl.BlockSpec((1, S, 2 * LANES), lambda b, h: (b, 0, h)),
            pl.BlockSpec((1, S, LANES), lambda b, h: (b, 0, h)),
            pl.BlockSpec((1, S, LANES), lambda b, h: (b, 0, h)),
        ],
        out_specs=pl.BlockSpec((1, S, 2 * LANES), lambda b, h: (b, 0, h)),
        out_shape=jax.ShapeDtypeStruct((B, S, ATT_Q_W), BF16),
        compiler_params=_cparams(2),
        name="attention",
    )(sink, q, k, v)


def _hgrn_chunk(q, g, k, v, st, tri, mask_ref, rev):
    C = CHUNK
    g_hi = g.astype(BF16)
    g_lo = (g - g_hi.astype(F32)).astype(BF16)
    b = _dot(tri, g_hi) + _dot(tri, g_lo)
    row = lax.broadcasted_iota(jnp.int32, (C, HG_DIM), 0)
    g_next = pltpu.roll(g, C - 1, axis=0)
    g_prev = pltpu.roll(g, 1, axis=0)
    scores = jnp.zeros((C, C), F32)
    for lv in range(CHUNK_LEVELS):
        half = 1 << lv
        bit = (row >> lv) & 1
        is_q = (bit == 0) if rev else (bit == 1)
        if lv == 0:
            e = jnp.where(is_q, g, 0.0)
        elif lv == 1:
            r = row & 3
            if rev:
                e = jnp.where(r == 0, g + g_next, jnp.where(r == 1, g, jnp.where(r == 2, 0.0, g_prev)))
            else:
                e = jnp.where(r == 0, g_next, jnp.where(r == 1, 0.0, jnp.where(r == 2, g, g + g_prev)))
        else:
            nblk = C // (2 * half)
            b3 = b.reshape(nblk, 2 * half, HG_DIM)
            ref_row = half if rev else half - 1
            bm = jnp.broadcast_to(b3[:, ref_row:ref_row + 1, :], (nblk, 2 * half, HG_DIM)).reshape(C, HG_DIM)
            d = b - bm
            e = jnp.where(is_q, d, -d)
        dec = jnp.exp(e)
        qt = jnp.where(is_q, q * dec, 0.0).astype(BF16)
        kt = jnp.where(is_q, 0.0, k * dec).astype(BF16)
        p = _dot_nt(qt, kt)
        if lv < CHUNK_LEVELS - 1:
            p = p * mask_ref[lv]
        scores = scores + p
    diag = jnp.sum(q * k, axis=-1, keepdims=True)
    b_end = b[0:1, :] if rev else b[C - 1:C, :]
    qh = (q * jnp.exp(b)).astype(BF16)
    kh = (k * jnp.exp(b_end - b)).astype(BF16)
    o = _dot(scores.astype(BF16), v) + diag * v.astype(F32) + _dot_nt(qh, st.astype(BF16))
    st_new = st * jnp.exp(b_end) + _dot_tn(v, kh)
    return o, st_new


def _hgrn_kernel(q_ref, lff_ref, kf_ref, lfb_ref, kb_ref, v_ref, g_ref, ng_ref, tri_ref, mask_ref,
                 o_ref, of_ref, *, n_ctx_chunks, n_lat_chunks):
    nc, nl = n_ctx_chunks, n_lat_chunks
    zero = jnp.zeros((HG_DIM, HG_DIM), F32)

    def fwd(j, st):
        r0 = pl.multiple_of(j * CHUNK, CHUNK)
        rows = pl.ds(r0, CHUNK)
        o, st = _hgrn_chunk(q_ref[0, rows, :], lff_ref[0, rows, :], kf_ref[0, rows, :], v_ref[0, rows, :],
                            st, tri_ref[0], mask_ref, False)
        of_ref[rows, :] = o
        return st

    lax.fori_loop(0, nc + nl, fwd, zero)

    def bwd(j, st):
        c = jnp.where(j < nc, nc - 1 - j, 2 * nc + nl - 1 - j)
        r0 = pl.multiple_of(c * CHUNK, CHUNK)
        rows = pl.ds(r0, CHUNK)
        o, st = _hgrn_chunk(q_ref[0, rows, :], lfb_ref[0, rows, :], kb_ref[0, rows, :], v_ref[0, rows, :],
                            st, tri_ref[1], mask_ref, True)
        tot = of_ref[rows, :] + o
        y = tot * lax.rsqrt(jnp.mean(tot * tot, axis=-1, keepdims=True) + RMS_EPS) * ng_ref[...]
        o_ref[0, rows, :] = (y * _silu(g_ref[0, rows, :])).astype(BF16)
        return st

    lax.fori_loop(0, nc + nl, bwd, zero)


def _hgrn(hq, lff, kf, lfb, kb, hi, hgate, norm_g, tri, masks, n_ctx):
    B, S, _ = hq.shape
    col = pl.BlockSpec((1, S, HG_DIM), lambda b, h: (b, 0, h))
    return pl.pallas_call(
        functools.partial(_hgrn_kernel, n_ctx_chunks=n_ctx // CHUNK, n_lat_chunks=(S - n_ctx) // CHUNK),
        grid=(B, HG_HEADS),
        in_specs=[col, col, col, col, col, col, col,
                  pl.BlockSpec((1, HG_DIM), lambda b, h: (0, 0)),
                  pl.BlockSpec(tri.shape, lambda b, h: (0, 0, 0)),
                  pl.BlockSpec(masks.shape, lambda b, h: (0, 0, 0))],
        out_specs=col,
        out_shape=jax.ShapeDtypeStruct((B, S, HG_W), BF16),
        scratch_shapes=[pltpu.VMEM((S, HG_DIM), F32)],
        compiler_params=_cparams(2),
        name="hgrn2",
    )(hq, lff, kf, lfb, kb, hi, hgate, norm_g, tri, masks)


def _mix_out_kernel(att_ref, hg_ref, x_ref, mod_ref, w_ref, lng_ref, lnb_ref, o_ref):
    y = _dot(att_ref[0], w_ref[0:ATT_Q_W, :]) + _dot(hg_ref[0], w_ref[ATT_Q_W:, :])
    gate = mod_ref[0, 0, 2:3, :]
    o_ref[0] = _layer_norm(DEEPNORM_ALPHA * x_ref[0] + gate * y, lng_ref[...], lnb_ref[...])


def _mix_out(att, hg, xs, mods, w_out, ln_g, ln_b, n_ctx_tiles, skip_ctx):
    B, S, _ = xs.shape
    off = n_ctx_tiles if skip_ctx else 0
    nt = S // TM - off
    rin = lambda w: pl.BlockSpec((1, TM, w), lambda b, i: (b, i + off, 0))
    vec = pl.BlockSpec((1, D_MODEL), lambda b, i: (0, 0))
    return pl.pallas_call(
        _mix_out_kernel,
        grid=(B, nt),
        in_specs=[rin(ATT_Q_W), rin(HG_W), rin(D_MODEL),
                  pl.BlockSpec((1, 1, 6, D_MODEL), lambda b, i: (b, jnp.where(i + off < n_ctx_tiles, 0, 1), 0, 0)),
                  _resident(w_out.shape), vec, vec],
        out_specs=pl.BlockSpec((1, TM, D_MODEL), lambda b, i: (b, i, 0)),
        out_shape=jax.ShapeDtypeStruct((B, nt * TM, D_MODEL), F32),
        compiler_params=_cparams(2),
        name="mix_out",
    )(att, hg, xs, mods, w_out, ln_g, ln_b)


def _halo_specs(halo):
    per = TM // halo

    def prev_map(b, i):
        return (b, jnp.maximum(i * per - 1, 0), 0)

    def make_next(n_tiles):
        def next_map(b, i):
            return (b, jnp.minimum((i + 1) * per, n_tiles * per - 1), 0)
        return next_map

    return (lambda: pl.BlockSpec((1, halo, D_MODEL), prev_map),
            lambda n_tiles: pl.BlockSpec((1, halo, D_MODEL), make_next(n_tiles)))


def _segment_edges(i, n_tiles, n_ctx_tiles):
    is_first = (i == 0) | (i == n_ctx_tiles)
    is_last = (i == n_tiles - 1) | (i == n_ctx_tiles - 1)
    return is_first, is_last


def _ffn_kernel(x_ref, xp_ref, xn_ref, mod_ref, wup_ref, dww_ref, dwb_ref, wdn_ref, lng_ref, lnb_ref,
                o_ref, *, n_tiles, n_ctx_tiles):
    i = pl.program_id(1)
    is_first, is_last = _segment_edges(i, n_tiles, n_ctx_tiles)
    shift = mod_ref[0, 0, 3:4, :]
    scale = mod_ref[0, 0, 4:5, :]
    gate_mod = mod_ref[0, 0, 5:6, :]
    x = x_ref[0]
    mod = lambda a: a * (1.0 + scale) + shift
    hp = jnp.where(is_first, 0.0, mod(xp_ref[0]))
    hn = jnp.where(is_last, 0.0, mod(xn_ref[0]))
    h_ext = jnp.concatenate([hp, mod(x), hn], axis=0).astype(BF16)
    n_ext = TM + 2 * FFN_HALO
    ug = _dot(h_ext, wup_ref[:, 0:FFN_HIDDEN])
    uv = _dot(mod(x).astype(BF16), wup_ref[:, FFN_HIDDEN:])
    lo = pltpu.roll(ug, 1, axis=0)[FFN_HALO:FFN_HALO + TM]
    hi = pltpu.roll(ug, n_ext - 1, axis=0)[FFN_HALO:FFN_HALO + TM]
    conv = (lo * dww_ref[0:1, :] + ug[FFN_HALO:FFN_HALO + TM] * dww_ref[1:2, :] + hi * dww_ref[2:3, :]
            + dwb_ref[...])
    act = (_silu(conv) * uv).astype(BF16)
    y = _dot(act, wdn_ref[...])
    o_ref[0] = _layer_norm(DEEPNORM_ALPHA * x + gate_mod * y, lng_ref[...], lnb_ref[...])


def _ffn(xs, mods, w_up, dw_w, dw_b, w_down, ln_g, ln_b, n_ctx_tiles):
    B, S, _ = xs.shape
    nt = S // TM
    prev_spec, next_spec = _halo_specs(FFN_HALO)
    vec = lambda w: pl.BlockSpec((1, w), lambda b, i: (0, 0))
    return pl.pallas_call(
        functools.partial(_ffn_kernel, n_tiles=nt, n_ctx_tiles=n_ctx_tiles),
        grid=(B, nt),
        in_specs=[pl.BlockSpec((1, TM, D_MODEL), lambda b, i: (b, i, 0)), prev_spec(), next_spec(nt),
                  pl.BlockSpec((1, 1, 6, D_MODEL), lambda b, i: (b, jnp.where(i < n_ctx_tiles, 0, 1), 0, 0)),
                  _resident(w_up.shape),
                  pl.BlockSpec(dw_w.shape, lambda b, i: (0, 0)), vec(FFN_HIDDEN),
                  _resident(w_down.shape), vec(D_MODEL), vec(D_MODEL)],
        out_specs=pl.BlockSpec((1, TM, D_MODEL), lambda b, i: (b, i, 0)),
        out_shape=jax.ShapeDtypeStruct((B, S, D_MODEL), F32),
        compiler_params=_cparams(2),
        name="conv_glu_ffn",
    )(xs, xs, xs, mods, w_up, dw_w, dw_b, w_down, ln_g, ln_b)


CONF_ROWS = 128


def _conf_kernel(x_ref, xp_ref, xn_ref, mod_ref, w1_ref, b1_ref, dww_ref, dwb_ref, cg_ref, cb_ref,
                 w2_ref, b2_ref, lng_ref, lnb_ref, o_ref, a_ref, c_ref, *, n_tiles, n_ctx_tiles):
    i = pl.program_id(1)
    is_first, is_last = _segment_edges(i, n_tiles, n_ctx_tiles)
    shift = mod_ref[0, 0, 0:1, :]
    scale = mod_ref[0, 0, 1:2, :]
    gate_mod = mod_ref[0, 0, 2:3, :]
    x = x_ref[0]
    n_ext = TM + 2 * CONF_HALO
    x_ext = jnp.concatenate([xp_ref[0], x, xn_ref[0]], axis=0)
    h_ext = (x_ext * (1.0 + scale) + shift).astype(BF16)
    a = _dot(h_ext, w1_ref[...]) + b1_ref[...]
    a = a[:, :D_MODEL] * _sigmoid(a[:, D_MODEL:])
    rows = lax.broadcasted_iota(jnp.int32, (n_ext, 1), 0)
    pad = ((rows < CONF_HALO) & is_first) | ((rows >= CONF_HALO + TM) & is_last)
    a_ref[...] = jnp.where(pad, 0.0, a)

    half = (CONF_KERNEL - 1) // 2
    base = CONF_HALO - half

    def lane_block(cb, carry):
        lanes = pl.ds(pl.multiple_of(cb * LANES, LANES), LANES)
        for rb in range(TM // CONF_ROWS):
            acc = jnp.broadcast_to(dwb_ref[:, lanes], (CONF_ROWS, LANES))
            for kk in range(CONF_KERNEL):
                acc = acc + a_ref[pl.ds(rb * CONF_ROWS + base + kk, CONF_ROWS), lanes] * dww_ref[kk:kk + 1, lanes]
            c_ref[pl.ds(rb * CONF_ROWS, CONF_ROWS), lanes] = acc
        return carry

    lax.fori_loop(0, D_MODEL // LANES, lane_block, 0)

    z = _silu(_layer_norm(c_ref[...], cg_ref[...], cb_ref[...])).astype(BF16)
    y = _dot(z, w2_ref[...]) + b2_ref[...]
    o_ref[0] = _layer_norm(DEEPNORM_ALPHA * x + gate_mod * y, lng_ref[...], lnb_ref[...])


def _conformer(xs, mods, w1, b1, dw_w, dw_b, cg, cb, w2, b2, ln_g, ln_b, n_ctx_tiles):
    B, S, _ = xs.shape
    nt = S // TM
    prev_spec, next_spec = _halo_specs(CONF_HALO)
    vec = lambda w: pl.BlockSpec((1, w), lambda b, i: (0, 0))
    return pl.pallas_call(
        functools.partial(_conf_kernel, n_tiles=nt, n_ctx_tiles=n_ctx_tiles),
        grid=(B, nt),
        in_specs=[pl.BlockSpec((1, TM, D_MODEL), lambda b, i: (b, i, 0)), prev_spec(), next_spec(nt),
                  pl.BlockSpec((1, 1, 6, D_MODEL), lambda b, i: (b, jnp.where(i < n_ctx_tiles, 0, 1), 0, 0)),
                  _resident(w1.shape), vec(2 * D_MODEL),
                  pl.BlockSpec(dw_w.shape, lambda b, i: (0, 0)), vec(D_MODEL), vec(D_MODEL), vec(D_MODEL),
                  _resident(w2.shape), vec(D_MODEL), vec(D_MODEL), vec(D_MODEL)],
        out_specs=pl.BlockSpec((1, TM, D_MODEL), lambda b, i: (b, i, 0)),
        out_shape=jax.ShapeDtypeStruct((B, S, D_MODEL), F32),
        scratch_shapes=[pltpu.VMEM((TM + 2 * CONF_HALO, D_MODEL), F32), pltpu.VMEM((TM, D_MODEL), F32)],
        compiler_params=_cparams(2),
        name="conformer_conv",
    )(xs, xs, xs, mods, w1, b1, dw_w, dw_b, cg, cb, w2, b2, ln_g, ln_b)


def _rope_tables(n_ctx, n_lat):
    rows = n_lat // GRID_W
    row = jnp.repeat(jnp.arange(rows), GRID_W).astype(F32)
    col = jnp.tile(jnp.arange(GRID_W), rows).astype(F32)
    quarter = ATT_HEAD_DIM // 4
    inv_freq = ROPE_BASE ** (-jnp.arange(quarter, dtype=F32) / quarter)
    ang_r = row[:, None] * inv_freq
    ang_c = col[:, None] * inv_freq
    ang = jnp.concatenate([ang_r, ang_r, ang_c, ang_c], axis=-1)
    ang = jnp.concatenate([jnp.zeros((n_ctx, ATT_HEAD_DIM), F32), ang], axis=0)
    ang = jnp.concatenate([ang, ang], axis=-1)
    cos, sin = jnp.cos(ang), jnp.sin(ang)
    low = (jnp.arange(LANES) % (2 * quarter)) < quarter
    return cos, jnp.where(low, -sin, 0.0), jnp.where(low, 0.0, sin)


def _chunk_constants():
    t = jnp.arange(CHUNK)
    lower = (t[None, :] <= t[:, None])
    tri = jnp.stack([lower, lower.T]).astype(BF16)
    masks = jnp.stack([(t[:, None] >> (lv + 1)) == (t[None, :] >> (lv + 1))
                       for lv in range(CHUNK_LEVELS - 1)]).astype(F32)
    return tri, masks


def kernel(x, c, ctx, c_ctx, mod_w, mod_b, post_ln_g, post_ln_b, mix_w_in, mix_w_out, att_sink, hg_lb_logits, hg_norm_g, conf_pw1_w, conf_pw1_b, conf_dw_w, conf_dw_b, conf_ln_g, conf_ln_b, conf_pw2_w, conf_pw2_b, ffn_w_up, ffn_dw_w, ffn_dw_b, ffn_w_down):
    B, T, D = x.shape
    L = ctx.shape[1]
    assert D == D_MODEL and L % TM == 0 and T % TM == 0 and L >= ATT_BLOCK and T % GRID_W == 0
    n_ctx_tiles = L // TM

    n_cond = -(-(B + 1) // SUBLANES) * SUBLANES
    cond = jnp.zeros((n_cond, D), F32).at[:B].set(c).at[B].set(c_ctx)
    mod = _modulation(cond, mod_w.astype(BF16), mod_b[:, None, :])
    mod_x = mod[:, :B].reshape(DEPTH, B, 1, 6, D)
    mod_c = jnp.broadcast_to(mod[:, B].reshape(DEPTH, 1, 1, 6, D), (DEPTH, B, 1, 6, D))
    mods = jnp.concatenate([mod_c, mod_x], axis=2)

    cos, sa, sb = _rope_tables(L, T)
    tri, masks = _chunk_constants()
    vec = lambda a: a.reshape(1, -1)

    xs = jnp.concatenate([ctx, x], axis=1)
    has_ctx = True
    for layer in range(DEPTH):
        m = layer // 2
        need_ctx_out = any(j % 2 == 0 for j in range(layer + 1, DEPTH))
        nct = n_ctx_tiles if has_ctx else 0
        lg, lb = post_ln_g[layer], post_ln_b[layer]
        if layer % 2 == 0:
            q, k, v, hq, lff, kf, lfb, kb, hi, hgate = _mix_in(
                xs, mods[layer], mix_w_in[m].astype(BF16), cos, sa, sb, hg_lb_logits, m, nct)
            att = _attention(att_sink[m], q, k, v, L, need_ctx_out)
            hg = _hgrn(hq, lff, kf, lfb, kb, hi, hgate, vec(hg_norm_g[m]), tri, masks, L)
            xs = _mix_out(att, hg, xs, mods[layer], mix_w_out[m].astype(BF16), vec(lg[0]), vec(lb[0]),
                          nct, not need_ctx_out)
            has_ctx = need_ctx_out
        else:
            xs = _conformer(xs, mods[layer], conf_pw1_w[m].astype(BF16), vec(conf_pw1_b[m]), conf_dw_w[m],
                            vec(conf_dw_b[m]), vec(conf_ln_g[m]), vec(conf_ln_b[m]),
                            conf_pw2_w[m].astype(BF16), vec(conf_pw2_b[m]), vec(lg[0]), vec(lb[0]), nct)
            if has_ctx and not need_ctx_out:
                xs = xs[:, L:]
                has_ctx = False
        nct = n_ctx_tiles if has_ctx else 0
        xs = _ffn(xs, mods[layer], ffn_w_up[layer].astype(BF16), ffn_dw_w[layer], vec(ffn_dw_b[layer]),
                  ffn_w_down[layer].astype(BF16), vec(lg[1]), vec(lb[1]), nct)
    return xs[:, L:] if has_ctx else xs
```

```python
import functools

import jax
import jax.numpy as jnp
from jax import lax
from jax.experimental import pallas as pl
from jax.experimental.pallas import tpu as pltpu

F32 = jnp.float32
BF16 = jnp.bfloat16

D_MODEL = 1024
DEPTH = 4
GRID_W = 64
ATT_HEADS = 8
ATT_KV_HEADS = 2
ATT_HEAD_DIM = 64
ATT_WINDOW = 128
ATT_BLOCK = 128
ROPE_BASE = 10000.0
HG_HEADS = 4
HG_DIM = 128
CONF_KERNEL = 31
FFN_HIDDEN = 2816
DEEPNORM_ALPHA = (2 * DEPTH) ** 0.25
LN_EPS = 1e-5
RMS_EPS = 1e-6
MASK_VALUE = -1e30
LB_FLOOR = 1e-30

ATT_Q_W = ATT_HEADS * ATT_HEAD_DIM
ATT_KV_W = ATT_KV_HEADS * ATT_HEAD_DIM
HG_W = HG_HEADS * HG_DIM
OFF_AQ = 0
OFF_AK = OFF_AQ + ATT_Q_W
OFF_AV = OFF_AK + ATT_KV_W
OFF_HQ = OFF_AV + ATT_KV_W
OFF_FF = OFF_HQ + HG_W
OFF_FB = OFF_FF + HG_W
OFF_HI = OFF_FB + HG_W
OFF_HGATE = OFF_HI + HG_W
MIX_IN_W = OFF_HGATE + HG_W

LANES = 128
SUBLANES = 8
TM = 256
CHUNK = 128
CHUNK_LEVELS = 7
FFN_HALO = SUBLANES
CONF_HALO = 16
VMEM_LIMIT = 56 * 1024 * 1024


def _cparams(n_grid):
    return pltpu.CompilerParams(
        dimension_semantics=("arbitrary",) * n_grid, vmem_limit_bytes=VMEM_LIMIT)


def _dot(a, b):
    return jnp.dot(a, b, preferred_element_type=F32)


def _dot_nt(a, b):
    return lax.dot_general(a, b, (((1,), (1,)), ((), ())), preferred_element_type=F32)


def _dot_tn(a, b):
    return lax.dot_general(a, b, (((0,), (0,)), ((), ())), preferred_element_type=F32)


def _sigmoid(x):
    e = jnp.exp(-jnp.abs(x))
    r = 1.0 / (1.0 + e)
    return jnp.where(x >= 0, r, e * r)


def _silu(x):
    return x * _sigmoid(x)


def _layer_norm(z, g, b):
    mu = jnp.mean(z, axis=-1, keepdims=True)
    zc = z - mu
    var = jnp.mean(zc * zc, axis=-1, keepdims=True)
    return zc * lax.rsqrt(var + LN_EPS) * g + b


def _resident(shape):
    nd = len(shape)
    return pl.BlockSpec(shape, lambda *_: (0,) * nd, pipeline_mode=pl.Buffered(1))


MOD_TN = 1536


def _mod_kernel(cond_ref, w_ref, b_ref, o_ref):
    s = _silu(cond_ref[...])
    hi = s.astype(BF16)
    lo = (s - hi.astype(F32)).astype(BF16)
    w = w_ref[0]
    o_ref[0] = _dot(hi, w) + _dot(lo, w) + b_ref[0]


def _modulation(cond, mod_w, mod_b):
    R = cond.shape[0]
    n6 = mod_w.shape[2]
    return pl.pallas_call(
        _mod_kernel,
        grid=(DEPTH, n6 // MOD_TN),
        in_specs=[
            pl.BlockSpec((R, D_MODEL), lambda l, j: (0, 0)),
            pl.BlockSpec((1, D_MODEL, MOD_TN), lambda l, j: (l, 0, j)),
            pl.BlockSpec((1, 1, MOD_TN), lambda l, j: (l, 0, j)),
        ],
        out_specs=pl.BlockSpec((1, R, MOD_TN), lambda l, j: (l, 0, j)),
        out_shape=jax.ShapeDtypeStruct((DEPTH, R, n6), F32),
        compiler_params=_cparams(2),
        name="modulation",
    )(cond, mod_w, mod_b)


def _mix_in_kernel(x_ref, mod_ref, w_ref, cos_ref, sa_ref, sb_ref, lbl_ref,
                   q_ref, k_ref, v_ref, hq_ref, lff_ref, kf_ref, lfb_ref, kb_ref, hi_ref, hg_ref,
                   *, layer_m):
    x = x_ref[0]
    shift = mod_ref[0, 0, 0:1, :]
    scale = mod_ref[0, 0, 1:2, :]
    h = (x * (1.0 + scale) + shift).astype(BF16)

    def proj(off, width):
        return _dot(h, w_ref[:, off:off + width])

    cos = cos_ref[...]
    sa = sa_ref[...]
    sb = sb_ref[...]

    def rope(a, reps):
        w = a.shape[1]
        c = jnp.concatenate([cos] * reps, axis=1) if reps > 1 else cos
        s1 = jnp.concatenate([sa] * reps, axis=1) if reps > 1 else sa
        s2 = jnp.concatenate([sb] * reps, axis=1) if reps > 1 else sb
        up = pltpu.roll(a, w - 16, axis=1)
        dn = pltpu.roll(a, 16, axis=1)
        return a * c + up * s1 + dn * s2

    q = rope(proj(OFF_AQ, ATT_Q_W), ATT_Q_W // LANES) * (ATT_HEAD_DIM ** -0.5)
    q_ref[0] = q.astype(BF16)

    lane = lax.broadcasted_iota(jnp.int32, (TM, LANES), 1)
    first = lane < ATT_HEAD_DIM

    def pair_rep(a):
        sw = pltpu.roll(a, ATT_HEAD_DIM, axis=1)
        return jnp.concatenate([jnp.where(first, a, sw), jnp.where(first, sw, a)], axis=1)

    k_ref[0] = pair_rep(rope(proj(OFF_AK, ATT_KV_W), 1)).astype(BF16)
    v_ref[0] = pair_rep(proj(OFF_AV, ATT_KV_W)).astype(BF16)

    hq_ref[0] = proj(OFF_HQ, HG_W)
    hi_ref[0] = proj(OFF_HI, HG_W).astype(BF16)
    hg_ref[0] = proj(OFF_HGATE, HG_W)

    logits = lbl_ref[...]
    n_mix = logits.shape[0]
    mx = logits[0:1, :]
    for r in range(1, n_mix):
        mx = jnp.maximum(mx, logits[r:r + 1, :])
    ex = [jnp.exp(logits[r:r + 1, :] - mx) for r in range(n_mix)]
    tot = ex[0]
    for r in range(1, n_mix):
        tot = tot + ex[r]
    cum = ex[0] / tot
    p0 = cum
    for r in range(1, layer_m + 1):
        cum = cum + ex[r] / tot
    lb = cum - p0
    lb_floor = jnp.maximum(lb, LB_FLOOR)
    one_m = 1.0 - lb

    def gates(fr, lf_out, k_out):
        e = jnp.exp(-jnp.abs(fr))
        r = 1.0 / (1.0 + e)
        er = e * r
        pos = fr >= 0
        lf_out[0] = jnp.log2(lb_floor + one_m * jnp.where(pos, r, er))
        k_out[0] = one_m * jnp.where(pos, er, r)

    gates(proj(OFF_FF, HG_W), lff_ref, kf_ref)
    gates(proj(OFF_FB, HG_W), lfb_ref, kb_ref)


def _mix_in(xs, mods, w_in, cos, sa, sb, lb_logits, layer_m, n_ctx_tiles):
    B, S, _ = xs.shape
    nt = S // TM
    row = lambda w: pl.BlockSpec((1, TM, w), lambda b, i: (b, i, 0))
    tab = pl.BlockSpec((TM, LANES), lambda b, i: (i, 0))
    sds = lambda w, dt: jax.ShapeDtypeStruct((B, S, w), dt)
    return pl.pallas_call(
        functools.partial(_mix_in_kernel, layer_m=layer_m),
        grid=(B, nt),
        in_specs=[
            row(D_MODEL),
            pl.BlockSpec((1, 1, 6, D_MODEL), lambda b, i: (b, jnp.where(i < n_ctx_tiles, 0, 1), 0, 0)),
            _resident((D_MODEL, MIX_IN_W)),
            tab, tab, tab,
            _resident(lb_logits.shape),
        ],
        out_specs=[row(ATT_Q_W), row(2 * LANES), row(2 * LANES), row(HG_W), row(HG_W), row(HG_W),
                   row(HG_W), row(HG_W), row(HG_W), row(HG_W)],
        out_shape=[sds(ATT_Q_W, BF16), sds(2 * LANES, BF16), sds(2 * LANES, BF16), sds(HG_W, F32),
                   sds(HG_W, F32), sds(HG_W, F32), sds(HG_W, F32), sds(HG_W, F32), sds(HG_W, BF16),
                   sds(HG_W, F32)],
        compiler_params=_cparams(2),
        name="mix_in",
    )(xs, mods, w_in, cos, sa, sb, lb_logits)


GROUP = ATT_HEADS // ATT_KV_HEADS


def _attn_kernel(sink_ref, q_ref, k_ref, v_ref, o_ref, *, n_ctx, n_lat, ctx_out):
    hkv = pl.program_id(1)
    QB = ATT_BLOCK
    nb = n_lat // QB
    lane = lax.broadcasted_iota(jnp.int32, (QB, LANES), 1)
    first = lane < ATT_HEAD_DIM
    rows4 = lax.broadcasted_iota(jnp.int32, (GROUP * QB, LANES), 0)
    rq = rows4 & (QB - 1)
    col = lax.broadcasted_iota(jnp.int32, (GROUP * QB, LANES), 1)
    grp = lax.broadcasted_iota(jnp.int32, (GROUP * QB, 1), 0) // QB
    sink = jnp.zeros((GROUP * QB, 1), F32)
    for g in range(GROUP):
        sink = jnp.where(grp == g, sink_ref[hkv * GROUP + g], sink)

    def stack_q(q):
        qa, qb = q[:, :LANES], q[:, LANES:]
        z = jnp.zeros_like(qa)
        return jnp.concatenate([jnp.where(first, qa, z), jnp.where(first, z, qa),
                                jnp.where(first, qb, z), jnp.where(first, z, qb)], axis=0)

    def unstack_o(o):
        return jnp.concatenate([jnp.where(first, o[0:QB], o[QB:2 * QB]),
                                jnp.where(first, o[2 * QB:3 * QB], o[3 * QB:4 * QB])], axis=1)

    kc = k_ref[0, 0:n_ctx, :]
    vc = v_ref[0, 0:n_ctx, :]

    def softmax_pv(parts):
        tiles = lambda a: [a[:, c:c + LANES] for c in range(0, a.shape[1], LANES)]
        mt = None
        for s, _ in parts:
            for t in tiles(s):
                mt = t if mt is None else jnp.maximum(mt, t)
        m = jnp.maximum(sink, jnp.max(mt, axis=-1, keepdims=True))
        dt = None
        acc = None
        for s, v in parts:
            p = jnp.exp(s - m)
            for t in tiles(p):
                dt = t if dt is None else dt + t
            pv = _dot(p.astype(BF16), v)
            acc = pv if acc is None else acc + pv
        den = jnp.exp(sink - m) + jnp.sum(dt, axis=-1, keepdims=True)
        return acc * (1.0 / den)

    def lat_block(i, carry):
        r0 = pl.multiple_of(n_ctx + i * QB, QB)
        rp = pl.multiple_of(r0 - QB, QB)
        rn = pl.multiple_of(jnp.minimum(r0 + QB, n_ctx + n_lat - QB), QB)
        q4 = stack_q(q_ref[0, pl.ds(r0, QB), :])
        lo_col = jnp.where(i > 0, rq, LANES)
        hi_col = jnp.where(i < nb - 1, rq, -1)
        sp = jnp.where(col >= lo_col, _dot_nt(q4, k_ref[0, pl.ds(rp, QB), :]), MASK_VALUE)
        ss = _dot_nt(q4, k_ref[0, pl.ds(r0, QB), :])
        sn = jnp.where(col <= hi_col, _dot_nt(q4, k_ref[0, pl.ds(rn, QB), :]), MASK_VALUE)
        sc = _dot_nt(q4, kc)
        o = softmax_pv([(sp, v_ref[0, pl.ds(rp, QB), :]), (ss, v_ref[0, pl.ds(r0, QB), :]),
                        (sn, v_ref[0, pl.ds(rn, QB), :]), (sc, vc)])
        o_ref[0, pl.ds(r0, QB), :] = unstack_o(o).astype(BF16)
        return carry

    lax.fori_loop(0, nb, lat_block, 0, unroll=2)

    for j in range(n_ctx // QB):
        if ctx_out:
            q4 = stack_q(q_ref[0, j * QB:(j + 1) * QB, :])
            o = softmax_pv([(_dot_nt(q4, kc), vc)])
            o_ref[0, j * QB:(j + 1) * QB, :] = unstack_o(o).astype(BF16)
        else:
            o_ref[0, j * QB:(j + 1) * QB, :] = jnp.zeros((QB, 2 * LANES), BF16)


def _attention(sink, q, k, v, n_ctx, ctx_out):
    B, S, _ = q.shape
    return pl.pallas_call(
        functools.partial(_attn_kernel, n_ctx=n_ctx, n_lat=S - n_ctx, ctx_out=ctx_out),
        grid=(B, ATT_KV_HEADS),
        in_specs=[
            pl.BlockSpec(memory_space=pltpu.SMEM),
            pl.BlockSpec((1, S, 2 * LANES), lambda b, h: (b, 0, h)),
            pl.BlockSpec((1, S, LANES), lambda b, h: (b, 0, h)),
            pl.BlockSpec((1, S, LANES), lambda b, h: (b, 0, h)),
        ],
        out_specs=pl.BlockSpec((1, S, 2 * LANES), lambda b, h: (b, 0, h)),
        out_shape=jax.ShapeDtypeStruct((B, S, ATT_Q_W), BF16),
        compiler_params=_cparams(2),
        name="attention",
    )(sink, q, k, v)


def _hgrn_chunk(q, g, k, v, st_ref, tri_ref, sgn_ref, mask_ref, coef_ref, d):
    C = CHUNK
    rev = d == 1
    g_hi = g.astype(BF16)
    g_lo = (g - g_hi.astype(F32)).astype(BF16)
    tri = tri_ref[d]
    b = _dot(tri, g_hi) + _dot(tri, g_lo)
    g_next = pltpu.roll(g, C - 1, axis=0)
    g_prev = pltpu.roll(g, 1, axis=0)
    scores = None
    for lv in range(CHUNK_LEVELS):
        half = 1 << lv
        sgn = sgn_ref[d, lv]
        if lv == 0:
            e = g * coef_ref[d, 0]
        elif lv == 1:
            e = g * coef_ref[d, 1] + g_next * coef_ref[d, 2] + g_prev * coef_ref[d, 3]
        else:
            nblk = C // (2 * half)
            b3 = b.reshape(nblk, 2 * half, HG_DIM)
            ref_row = half if rev else half - 1
            bm = jnp.broadcast_to(b3[:, ref_row:ref_row + 1, :], (nblk, 2 * half, HG_DIM)).reshape(C, HG_DIM)
            e = (b - bm) * sgn
        x = (jnp.where(sgn > 0, q, k) * jnp.exp2(e)).astype(BF16)
        p = _dot_nt(x, x) * mask_ref[d, lv]
        scores = p if scores is None else scores + p
    diag = jnp.sum(q * k, axis=-1, keepdims=True)
    b_end = b[0:1, :] if rev else b[C - 1:C, :]
    qh = (q * jnp.exp2(b)).astype(BF16)
    kh = (k * jnp.exp2(b_end - b)).astype(BF16)
    st = st_ref[d]
    o = _dot(scores.astype(BF16), v) + diag * v.astype(F32) + _dot_nt(qh, st.astype(BF16))
    st_ref[d] = st * jnp.exp2(b_end) + _dot_tn(v, kh)
    return o


def _hgrn_kernel(q_ref, lff_ref, kf_ref, lfb_ref, kb_ref, v_ref, g_ref, ng_ref, tri_ref, sgn_ref, mask_ref,
                 coef_ref, o_ref, of_ref, ob_ref, st_ref, *, n_ctx_chunks, n_lat_chunks):
    nc, nl = n_ctx_chunks, n_lat_chunks
    st_ref[...] = jnp.zeros(st_ref.shape, F32)

    def step(j, carry):
        rf = pl.ds(pl.multiple_of(j * CHUNK, CHUNK), CHUNK)
        of_ref[rf, :] = _hgrn_chunk(q_ref[0, rf, :], lff_ref[0, rf, :], kf_ref[0, rf, :], v_ref[0, rf, :],
                                    st_ref, tri_ref, sgn_ref, mask_ref, coef_ref, 0)
        cb = jnp.where(j < nc, nc - 1 - j, 2 * nc + nl - 1 - j)
        rb = pl.ds(pl.multiple_of(cb * CHUNK, CHUNK), CHUNK)
        ob_ref[rb, :] = _hgrn_chunk(q_ref[0, rb, :], lfb_ref[0, rb, :], kb_ref[0, rb, :], v_ref[0, rb, :],
                                    st_ref, tri_ref, sgn_ref, mask_ref, coef_ref, 1)
        return carry

    lax.fori_loop(0, nc + nl, step, 0, unroll=2)

    def readout(j, carry):
        rows = pl.ds(pl.multiple_of(j * CHUNK, CHUNK), CHUNK)
        tot = of_ref[rows, :] + ob_ref[rows, :]
        y = tot * lax.rsqrt(jnp.mean(tot * tot, axis=-1, keepdims=True) + RMS_EPS) * ng_ref[...]
        o_ref[0, rows, :] = (y * _silu(g_ref[0, rows, :])).astype(BF16)
        return carry

    lax.fori_loop(0, nc + nl, readout, 0)


def _hgrn(hq, lff, kf, lfb, kb, hi, hgate, norm_g, consts, n_ctx):
    B, S, _ = hq.shape
    col = pl.BlockSpec((1, S, HG_DIM), lambda b, h: (b, 0, h))
    return pl.pallas_call(
        functools.partial(_hgrn_kernel, n_ctx_chunks=n_ctx // CHUNK, n_lat_chunks=(S - n_ctx) // CHUNK),
        grid=(B, HG_HEADS),
        in_specs=[col, col, col, col, col, col, col,
                  pl.BlockSpec((1, HG_DIM), lambda b, h: (0, 0))] + [_resident(a.shape) for a in consts],
        out_specs=col,
        out_shape=jax.ShapeDtypeStruct((B, S, HG_W), BF16),
        scratch_shapes=[pltpu.VMEM((S, HG_DIM), F32), pltpu.VMEM((S, HG_DIM), F32),
                        pltpu.VMEM((2, HG_DIM, HG_DIM), F32)],
        compiler_params=_cparams(2),
        name="hgrn2",
    )(hq, lff, kf, lfb, kb, hi, hgate, norm_g, *consts)


def _mix_out_kernel(att_ref, hg_ref, x_ref, mod_ref, w_ref, lng_ref, lnb_ref, o_ref):
    y = _dot(att_ref[0], w_ref[0:ATT_Q_W, :]) + _dot(hg_ref[0], w_ref[ATT_Q_W:, :])
    gate = mod_ref[0, 0, 2:3, :]
    o_ref[0] = _layer_norm(DEEPNORM_ALPHA * x_ref[0] + gate * y, lng_ref[...], lnb_ref[...])


def _mix_out(att, hg, xs, mods, w_out, ln_g, ln_b, n_ctx_tiles, skip_ctx):
    B, S, _ = xs.shape
    off = n_ctx_tiles if skip_ctx else 0
    nt = S // TM - off
    rin = lambda w: pl.BlockSpec((1, TM, w), lambda b, i: (b, i + off, 0))
    vec = pl.BlockSpec((1, D_MODEL), lambda b, i: (0, 0))
    return pl.pallas_call(
        _mix_out_kernel,
        grid=(B, nt),
        in_specs=[rin(ATT_Q_W), rin(HG_W), rin(D_MODEL),
                  pl.BlockSpec((1, 1, 6, D_MODEL), lambda b, i: (b, jnp.where(i + off < n_ctx_tiles, 0, 1), 0, 0)),
                  _resident(w_out.shape), vec, vec],
        out_specs=pl.BlockSpec((1, TM, D_MODEL), lambda b, i: (b, i, 0)),
        out_shape=jax.ShapeDtypeStruct((B, nt * TM, D_MODEL), F32),
        compiler_params=_cparams(2),
        name="mix_out",
    )(att, hg, xs, mods, w_out, ln_g, ln_b)


def _halo_specs(halo):
    per = TM // halo

    def prev_map(b, i):
        return (b, jnp.maximum(i * per - 1, 0), 0)

    def make_next(n_tiles):
        def next_map(b, i):
            return (b, jnp.minimum((i + 1) * per, n_tiles * per - 1), 0)
        return next_map

    return (lambda: pl.BlockSpec((1, halo, D_MODEL), prev_map),
            lambda n_tiles: pl.BlockSpec((1, halo, D_MODEL), make_next(n_tiles)))


def _segment_edges(i, n_tiles, n_ctx_tiles):
    is_first = (i == 0) | (i == n_ctx_tiles)
    is_last = (i == n_tiles - 1) | (i == n_ctx_tiles - 1)
    return is_first, is_last


def _ffn_kernel(x_ref, xp_ref, xn_ref, mod_ref, wup_ref, dww_ref, dwb_ref, wdn_ref, lng_ref, lnb_ref,
                o_ref, *, n_tiles, n_ctx_tiles):
    i = pl.program_id(1)
    is_first, is_last = _segment_edges(i, n_tiles, n_ctx_tiles)
    shift = mod_ref[0, 0, 3:4, :]
    scale = mod_ref[0, 0, 4:5, :]
    gate_mod = mod_ref[0, 0, 5:6, :]
    x = x_ref[0]
    mod = lambda a: a * (1.0 + scale) + shift
    hp = jnp.where(is_first, 0.0, mod(xp_ref[0]))
    hn = jnp.where(is_last, 0.0, mod(xn_ref[0]))
    h_ext = jnp.concatenate([hp, mod(x), hn], axis=0).astype(BF16)
    n_ext = TM + 2 * FFN_HALO
    ug = _dot(h_ext, wup_ref[:, 0:FFN_HIDDEN])
    uv = _dot(mod(x).astype(BF16), wup_ref[:, FFN_HIDDEN:])
    lo = pltpu.roll(ug, 1, axis=0)[FFN_HALO:FFN_HALO + TM]
    hi = pltpu.roll(ug, n_ext - 1, axis=0)[FFN_HALO:FFN_HALO + TM]
    conv = (lo * dww_ref[0:1, :] + ug[FFN_HALO:FFN_HALO + TM] * dww_ref[1:2, :] + hi * dww_ref[2:3, :]
            + dwb_ref[...])
    act = (_silu(conv) * uv).astype(BF16)
    y = _dot(act, wdn_ref[...])
    o_ref[0] = _layer_norm(DEEPNORM_ALPHA * x + gate_mod * y, lng_ref[...], lnb_ref[...])


def _ffn(xs, mods, w_up, dw_w, dw_b, w_down, ln_g, ln_b, n_ctx_tiles):
    B, S, _ = xs.shape
    nt = S // TM
    prev_spec, next_spec = _halo_specs(FFN_HALO)
    vec = lambda w: pl.BlockSpec((1, w), lambda b, i: (0, 0))
    return pl.pallas_call(
        functools.partial(_ffn_kernel, n_tiles=nt, n_ctx_tiles=n_ctx_tiles),
        grid=(B, nt),
        in_specs=[pl.BlockSpec((1, TM, D_MODEL), lambda b, i: (b, i, 0)), prev_spec(), next_spec(nt),
                  pl.BlockSpec((1, 1, 6, D_MODEL), lambda b, i: (b, jnp.where(i < n_ctx_tiles, 0, 1), 0, 0)),
                  _resident(w_up.shape),
                  pl.BlockSpec(dw_w.shape, lambda b, i: (0, 0)), vec(FFN_HIDDEN),
                  _resident(w_down.shape), vec(D_MODEL), vec(D_MODEL)],
        out_specs=pl.BlockSpec((1, TM, D_MODEL), lambda b, i: (b, i, 0)),
        out_shape=jax.ShapeDtypeStruct((B, S, D_MODEL), F32),
        compiler_params=_cparams(2),
        name="conv_glu_ffn",
    )(xs, xs, xs, mods, w_up, dw_w, dw_b, w_down, ln_g, ln_b)


CONF_ROWS = 128


def _conf_kernel(x_ref, xp_ref, xn_ref, mod_ref, w1_ref, b1_ref, dww_ref, dwb_ref, cg_ref, cb_ref,
                 w2_ref, b2_ref, lng_ref, lnb_ref, o_ref, a_ref, c_ref, *, n_tiles, n_ctx_tiles):
    i = pl.program_id(1)
    is_first, is_last = _segment_edges(i, n_tiles, n_ctx_tiles)
    shift = mod_ref[0, 0, 0:1, :]
    scale = mod_ref[0, 0, 1:2, :]
    gate_mod = mod_ref[0, 0, 2:3, :]
    x = x_ref[0]
    n_ext = TM + 2 * CONF_HALO
    x_ext = jnp.concatenate([xp_ref[0], x, xn_ref[0]], axis=0)
    h_ext = (x_ext * (1.0 + scale) + shift).astype(BF16)
    a = _dot(h_ext, w1_ref[...]) + b1_ref[...]
    a = a[:, :D_MODEL] * _sigmoid(a[:, D_MODEL:])
    rows = lax.broadcasted_iota(jnp.int32, (n_ext, 1), 0)
    pad = ((rows < CONF_HALO) & is_first) | ((rows >= CONF_HALO + TM) & is_last)
    a_ref[...] = jnp.where(pad, 0.0, a)

    half = (CONF_KERNEL - 1) // 2
    base = CONF_HALO - half
    n_slab = CONF_ROWS + 2 * CONF_HALO

    def lane_block(cb, carry):
        lanes = pl.ds(pl.multiple_of(cb * LANES, LANES), LANES)
        for rb in range(TM // CONF_ROWS):
            slab = a_ref[pl.ds(rb * CONF_ROWS, n_slab), lanes]
            acc = jnp.broadcast_to(dwb_ref[:, lanes], (CONF_ROWS, LANES))
            for r in range(SUBLANES):
                sh = pltpu.roll(slab, n_slab - r, axis=0) if r else slab
                for mm in range(-(-(base + CONF_KERNEL) // SUBLANES)):
                    kk = SUBLANES * mm + r - base
                    if 0 <= kk < CONF_KERNEL:
                        acc = acc + sh[SUBLANES * mm:SUBLANES * mm + CONF_ROWS] * dww_ref[kk:kk + 1, lanes]
            c_ref[pl.ds(rb * CONF_ROWS, CONF_ROWS), lanes] = acc
        return carry

    lax.fori_loop(0, D_MODEL // LANES, lane_block, 0)

    z = _silu(_layer_norm(c_ref[...], cg_ref[...], cb_ref[...])).astype(BF16)
    y = _dot(z, w2_ref[...]) + b2_ref[...]
    o_ref[0] = _layer_norm(DEEPNORM_ALPHA * x + gate_mod * y, lng_ref[...], lnb_ref[...])


def _conformer(xs, mods, w1, b1, dw_w, dw_b, cg, cb, w2, b2, ln_g, ln_b, n_ctx_tiles):
    B, S, _ = xs.shape
    nt = S // TM
    prev_spec, next_spec = _halo_specs(CONF_HALO)
    vec = lambda w: pl.BlockSpec((1, w), lambda b, i: (0, 0))
    return pl.pallas_call(
        functools.partial(_conf_kernel, n_tiles=nt, n_ctx_tiles=n_ctx_tiles),
        grid=(B, nt),
        in_specs=[pl.BlockSpec((1, TM, D_MODEL), lambda b, i: (b, i, 0)), prev_spec(), next_spec(nt),
                  pl.BlockSpec((1, 1, 6, D_MODEL), lambda b, i: (b, jnp.where(i < n_ctx_tiles, 0, 1), 0, 0)),
                  _resident(w1.shape), vec(2 * D_MODEL),
                  pl.BlockSpec(dw_w.shape, lambda b, i: (0, 0)), vec(D_MODEL), vec(D_MODEL), vec(D_MODEL),
                  _resident(w2.shape), vec(D_MODEL), vec(D_MODEL), vec(D_MODEL)],
        out_specs=pl.BlockSpec((1, TM, D_MODEL), lambda b, i: (b, i, 0)),
        out_shape=jax.ShapeDtypeStruct((B, S, D_MODEL), F32),
        scratch_shapes=[pltpu.VMEM((TM + 2 * CONF_HALO, D_MODEL), F32), pltpu.VMEM((TM, D_MODEL), F32)],
        compiler_params=_cparams(2),
        name="conformer_conv",
    )(xs, xs, xs, mods, w1, b1, dw_w, dw_b, cg, cb, w2, b2, ln_g, ln_b)


def _rope_tables(n_ctx, n_lat):
    rows = n_lat // GRID_W
    row = jnp.repeat(jnp.arange(rows), GRID_W).astype(F32)
    col = jnp.tile(jnp.arange(GRID_W), rows).astype(F32)
    quarter = ATT_HEAD_DIM // 4
    inv_freq = ROPE_BASE ** (-jnp.arange(quarter, dtype=F32) / quarter)
    ang_r = row[:, None] * inv_freq
    ang_c = col[:, None] * inv_freq
    ang = jnp.concatenate([ang_r, ang_r, ang_c, ang_c], axis=-1)
    ang = jnp.concatenate([jnp.zeros((n_ctx, ATT_HEAD_DIM), F32), ang], axis=0)
    ang = jnp.concatenate([ang, ang], axis=-1)
    cos, sin = jnp.cos(ang), jnp.sin(ang)
    low = (jnp.arange(LANES) % (2 * quarter)) < quarter
    return cos, jnp.where(low, -sin, 0.0), jnp.where(low, 0.0, sin)


def _chunk_constants():
    t = jnp.arange(CHUNK)
    lower = t[None, :] <= t[:, None]
    tri = jnp.stack([lower, lower.T]).astype(BF16)
    full = lambda col: jnp.broadcast_to(col[:, None], (CHUNK, HG_DIM)).astype(F32)
    sgn, mask = [], []
    for d in range(2):
        sgn_d, mask_d = [], []
        for lv in range(CHUNK_LEVELS):
            is_q = ((t >> lv) & 1) == (1 - d)
            same = (t[:, None] >> (lv + 1)) == (t[None, :] >> (lv + 1))
            sgn_d.append(full(jnp.where(is_q, 1.0, -1.0)))
            mask_d.append((same & is_q[:, None] & ~is_q[None, :]).astype(F32))
        sgn.append(jnp.stack(sgn_d))
        mask.append(jnp.stack(mask_d))
    r = t & 3
    pick = lambda *rs: full(sum((r == x) for x in rs) > 0)
    coef = jnp.stack([
        jnp.stack([full((t & 1) == 1), pick(2, 3), pick(0), pick(3)]),
        jnp.stack([full((t & 1) == 0), pick(0, 1), pick(0), pick(3)]),
    ])
    return tri, jnp.stack(sgn), jnp.stack(mask), coef


def kernel(x, c, ctx, c_ctx, mod_w, mod_b, post_ln_g, post_ln_b, mix_w_in, mix_w_out, att_sink, hg_lb_logits, hg_norm_g, conf_pw1_w, conf_pw1_b, conf_dw_w, conf_dw_b, conf_ln_g, conf_ln_b, conf_pw2_w, conf_pw2_b, ffn_w_up, ffn_dw_w, ffn_dw_b, ffn_w_down):
    B, T, D = x.shape
    L = ctx.shape[1]
    assert D == D_MODEL and L % TM == 0 and T % TM == 0 and L >= ATT_BLOCK and T % GRID_W == 0
    n_ctx_tiles = L // TM

    n_cond = -(-(B + 1) // SUBLANES) * SUBLANES
    cond = jnp.zeros((n_cond, D), F32).at[:B].set(c).at[B].set(c_ctx)
    mod = _modulation(cond, mod_w.astype(BF16), mod_b[:, None, :])
    mod_x = mod[:, :B].reshape(DEPTH, B, 1, 6, D)
    mod_c = jnp.broadcast_to(mod[:, B].reshape(DEPTH, 1, 1, 6, D), (DEPTH, B, 1, 6, D))
    mods = jnp.concatenate([mod_c, mod_x], axis=2)

    cos, sa, sb = _rope_tables(L, T)
    chunk_consts = _chunk_constants()
    vec = lambda a: a.reshape(1, -1)

    xs = jnp.concatenate([ctx, x], axis=1)
    has_ctx = True
    for layer in range(DEPTH):
        m = layer // 2
        need_ctx_out = any(j % 2 == 0 for j in range(layer + 1, DEPTH))
        nct = n_ctx_tiles if has_ctx else 0
        lg, lb = post_ln_g[layer], post_ln_b[layer]
        if layer % 2 == 0:
            q, k, v, hq, lff, kf, lfb, kb, hi, hgate = _mix_in(
                xs, mods[layer], mix_w_in[m].astype(BF16), cos, sa, sb, hg_lb_logits, m, nct)
            att = _attention(att_sink[m], q, k, v, L, need_ctx_out)
            hg = _hgrn(hq, lff, kf, lfb, kb, hi, hgate, vec(hg_norm_g[m]), chunk_consts, L)
            xs = _mix_out(att, hg, xs, mods[layer], mix_w_out[m].astype(BF16), vec(lg[0]), vec(lb[0]),
                          nct, not need_ctx_out)
            has_ctx = need_ctx_out
        else:
            xs = _conformer(xs, mods[layer], conf_pw1_w[m].astype(BF16), vec(conf_pw1_b[m]), conf_dw_w[m],
                            vec(conf_dw_b[m]), vec(conf_ln_g[m]), vec(conf_ln_b[m]),
                            conf_pw2_w[m].astype(BF16), vec(conf_pw2_b[m]), vec(lg[0]), vec(lb[0]), nct)
            if has_ctx and not need_ctx_out:
                xs = xs[:, L:]
                has_ctx = False
        nct = n_ctx_tiles if has_ctx else 0
        xs = _ffn(xs, mods[layer], ffn_w_up[layer].astype(BF16), ffn_dw_w[layer], vec(ffn_dw_b[layer]),
                  ffn_w_down[layer].astype(BF16), vec(lg[1]), vec(lb[1]), nct)
    return xs[:, L:] if has_ctx else xs
```

```python
import functools

import jax
import jax.numpy as jnp
from jax import lax
from jax.experimental import pallas as pl
from jax.experimental.pallas import tpu as pltpu

F32 = jnp.float32
BF16 = jnp.bfloat16

D_MODEL = 1024
DEPTH = 4
GRID_W = 64
ATT_HEADS = 8
ATT_KV_HEADS = 2
ATT_HEAD_DIM = 64
ATT_WINDOW = 128
ATT_BLOCK = 128
ROPE_BASE = 10000.0
HG_HEADS = 4
HG_DIM = 128
CONF_KERNEL = 31
FFN_HIDDEN = 2816
DEEPNORM_ALPHA = (2 * DEPTH) ** 0.25
LN_EPS = 1e-5
RMS_EPS = 1e-6
MASK_VALUE = -1e30
LB_FLOOR = 1e-30

ATT_Q_W = ATT_HEADS * ATT_HEAD_DIM
ATT_KV_W = ATT_KV_HEADS * ATT_HEAD_DIM
HG_W = HG_HEADS * HG_DIM
OFF_AQ = 0
OFF_AK = OFF_AQ + ATT_Q_W
OFF_AV = OFF_AK + ATT_KV_W
OFF_HQ = OFF_AV + ATT_KV_W
OFF_FF = OFF_HQ + HG_W
OFF_FB = OFF_FF + HG_W
OFF_HI = OFF_FB + HG_W
OFF_HGATE = OFF_HI + HG_W
MIX_IN_W = OFF_HGATE + HG_W

LANES = 128
SUBLANES = 8
MXU_N = 256
TM = 256
CHUNK = 128
CHUNK_LEVELS = 7
FFN_HALO = SUBLANES
CONF_HALO = 16
VMEM_LIMIT = 56 * 1024 * 1024


def _cparams(n_grid):
    return pltpu.CompilerParams(
        dimension_semantics=("arbitrary",) * n_grid, vmem_limit_bytes=VMEM_LIMIT)


def _dot(a, b):
    return jnp.dot(a, b, preferred_element_type=F32)


def _dot_nt(a, b):
    return lax.dot_general(a, b, (((1,), (1,)), ((), ())), preferred_element_type=F32)


def _dot_tn(a, b):
    return lax.dot_general(a, b, (((0,), (0,)), ((), ())), preferred_element_type=F32)


def _sigmoid(x):
    e = jnp.exp(-jnp.abs(x))
    r = 1.0 / (1.0 + e)
    return jnp.where(x >= 0, r, e * r)


def _silu(x):
    return x * _sigmoid(x)


def _layer_norm(z, g, b):
    mu = jnp.mean(z, axis=-1, keepdims=True)
    zc = z - mu
    var = jnp.mean(zc * zc, axis=-1, keepdims=True)
    return zc * lax.rsqrt(var + LN_EPS) * g + b


def _resident(shape):
    nd = len(shape)
    return pl.BlockSpec(shape, lambda *_: (0,) * nd, pipeline_mode=pl.Buffered(1))


MOD_TN = 1536


def _mod_kernel(cond_ref, w_ref, b_ref, o_ref):
    s = _silu(cond_ref[...])
    hi = s.astype(BF16)
    lo = (s - hi.astype(F32)).astype(BF16)
    w = w_ref[0]
    o_ref[0] = _dot(hi, w) + _dot(lo, w) + b_ref[0]


def _modulation(cond, mod_w, mod_b):
    R = cond.shape[0]
    n6 = mod_w.shape[2]
    return pl.pallas_call(
        _mod_kernel,
        grid=(DEPTH, n6 // MOD_TN),
        in_specs=[
            pl.BlockSpec((R, D_MODEL), lambda l, j: (0, 0)),
            pl.BlockSpec((1, D_MODEL, MOD_TN), lambda l, j: (l, 0, j)),
            pl.BlockSpec((1, 1, MOD_TN), lambda l, j: (l, 0, j)),
        ],
        out_specs=pl.BlockSpec((1, R, MOD_TN), lambda l, j: (l, 0, j)),
        out_shape=jax.ShapeDtypeStruct((DEPTH, R, n6), F32),
        compiler_params=_cparams(2),
        name="modulation",
    )(cond, mod_w, mod_b)


def _mix_in_kernel(x_ref, mod_ref, w_ref, cos_ref, sa_ref, sb_ref, lbl_ref, tri_ref,
                   q_ref, k_ref, v_ref, hq_ref, bf_ref, kf_ref, bb_ref, kb_ref, hi_ref, hg_ref,
                   *, layer_m):
    x = x_ref[0]
    shift = mod_ref[0, 0, 0:1, :]
    scale = mod_ref[0, 0, 1:2, :]
    h = (x * (1.0 + scale) + shift).astype(BF16)

    def proj(off, width):
        return _dot(h, w_ref[:, off:off + width])

    cos = cos_ref[...]
    sa = sa_ref[...]
    sb = sb_ref[...]

    def rope(a, reps):
        w = a.shape[1]
        c = jnp.concatenate([cos] * reps, axis=1) if reps > 1 else cos
        s1 = jnp.concatenate([sa] * reps, axis=1) if reps > 1 else sa
        s2 = jnp.concatenate([sb] * reps, axis=1) if reps > 1 else sb
        up = pltpu.roll(a, w - 16, axis=1)
        dn = pltpu.roll(a, 16, axis=1)
        return a * c + up * s1 + dn * s2

    q = rope(proj(OFF_AQ, ATT_Q_W), ATT_Q_W // LANES) * (ATT_HEAD_DIM ** -0.5)
    q_ref[0] = q.astype(BF16)

    lane = lax.broadcasted_iota(jnp.int32, (TM, LANES), 1)
    first = lane < ATT_HEAD_DIM

    def pair_rep(a):
        sw = pltpu.roll(a, ATT_HEAD_DIM, axis=1)
        return jnp.concatenate([jnp.where(first, a, sw), jnp.where(first, sw, a)], axis=1)

    k_ref[0] = pair_rep(rope(proj(OFF_AK, ATT_KV_W), 1)).astype(BF16)
    v_ref[0] = pair_rep(proj(OFF_AV, ATT_KV_W)).astype(BF16)

    hq_ref[0] = proj(OFF_HQ, HG_W)
    hi_ref[0] = proj(OFF_HI, HG_W).astype(BF16)
    hg_ref[0] = proj(OFF_HGATE, HG_W)

    logits = lbl_ref[...]
    n_mix = logits.shape[0]
    mx = logits[0:1, :]
    for r in range(1, n_mix):
        mx = jnp.maximum(mx, logits[r:r + 1, :])
    ex = [jnp.exp(logits[r:r + 1, :] - mx) for r in range(n_mix)]
    tot = ex[0]
    for r in range(1, n_mix):
        tot = tot + ex[r]
    cum = ex[0] / tot
    p0 = cum
    for r in range(1, layer_m + 1):
        cum = cum + ex[r] / tot
    lb = cum - p0
    lb_floor = jnp.maximum(lb, LB_FLOOR)
    one_m = 1.0 - lb

    def gates(fr, d, b_out, k_out):
        e = jnp.exp(-jnp.abs(fr))
        r = 1.0 / (1.0 + e)
        er = e * r
        pos = fr >= 0
        k_out[0] = one_m * jnp.where(pos, er, r)
        lf = jnp.log2(lb_floor + one_m * jnp.where(pos, r, er))
        tri = tri_ref[d]
        for c in range(TM // CHUNK):
            g = lf[c * CHUNK:(c + 1) * CHUNK]
            g_hi = g.astype(BF16)
            g_lo = (g - g_hi.astype(F32)).astype(BF16)
            b_out[0, c * CHUNK:(c + 1) * CHUNK, :] = _dot(tri, g_hi) + _dot(tri, g_lo)

    gates(proj(OFF_FF, HG_W), 0, bf_ref, kf_ref)
    gates(proj(OFF_FB, HG_W), 1, bb_ref, kb_ref)


def _mix_in(xs, mods, w_in, cos, sa, sb, lb_logits, tri, layer_m, n_ctx_tiles):
    B, S, _ = xs.shape
    nt = S // TM
    row = lambda w: pl.BlockSpec((1, TM, w), lambda b, i: (b, i, 0))
    tab = pl.BlockSpec((TM, LANES), lambda b, i: (i, 0))
    sds = lambda w, dt: jax.ShapeDtypeStruct((B, S, w), dt)
    return pl.pallas_call(
        functools.partial(_mix_in_kernel, layer_m=layer_m),
        grid=(B, nt),
        in_specs=[
            row(D_MODEL),
            pl.BlockSpec((1, 1, 6, D_MODEL), lambda b, i: (b, jnp.where(i < n_ctx_tiles, 0, 1), 0, 0)),
            _resident((D_MODEL, MIX_IN_W)),
            tab, tab, tab,
            _resident(lb_logits.shape),
            _resident(tri.shape),
        ],
        out_specs=[row(ATT_Q_W), row(2 * LANES), row(2 * LANES), row(HG_W), row(HG_W), row(HG_W),
                   row(HG_W), row(HG_W), row(HG_W), row(HG_W)],
        out_shape=[sds(ATT_Q_W, BF16), sds(2 * LANES, BF16), sds(2 * LANES, BF16), sds(HG_W, F32),
                   sds(HG_W, F32), sds(HG_W, F32), sds(HG_W, F32), sds(HG_W, F32), sds(HG_W, BF16),
                   sds(HG_W, F32)],
        compiler_params=_cparams(2),
        name="mix_in",
    )(xs, mods, w_in, cos, sa, sb, lb_logits, tri)


GROUP = ATT_HEADS // ATT_KV_HEADS


def _attn_kernel(sink_ref, q_ref, k_ref, v_ref, o_ref, *, n_ctx, n_lat, ctx_out):
    hkv = pl.program_id(1)
    QB = ATT_BLOCK
    nb = n_lat // QB
    lane = lax.broadcasted_iota(jnp.int32, (QB, LANES), 1)
    first = lane < ATT_HEAD_DIM
    rows4 = lax.broadcasted_iota(jnp.int32, (GROUP * QB, LANES), 0)
    rq = rows4 & (QB - 1)
    col = lax.broadcasted_iota(jnp.int32, (GROUP * QB, LANES), 1)
    grp = lax.broadcasted_iota(jnp.int32, (GROUP * QB, 1), 0) // QB
    sink = jnp.zeros((GROUP * QB, 1), F32)
    for g in range(GROUP):
        sink = jnp.where(grp == g, sink_ref[hkv * GROUP + g], sink)

    def stack_q(q):
        qa, qb = q[:, :LANES], q[:, LANES:]
        z = jnp.zeros_like(qa)
        return jnp.concatenate([jnp.where(first, qa, z), jnp.where(first, z, qa),
                                jnp.where(first, qb, z), jnp.where(first, z, qb)], axis=0)

    def unstack_o(o):
        return jnp.concatenate([jnp.where(first, o[0:QB], o[QB:2 * QB]),
                                jnp.where(first, o[2 * QB:3 * QB], o[3 * QB:4 * QB])], axis=1)

    kc = k_ref[0, 0:n_ctx, :]
    vc = v_ref[0, 0:n_ctx, :]

    def softmax_pv(parts):
        tiles = lambda a: [a[:, c:c + LANES] for c in range(0, a.shape[1], LANES)]
        mt = None
        for s, _ in parts:
            for t in tiles(s):
                mt = t if mt is None else jnp.maximum(mt, t)
        m = jnp.maximum(sink, jnp.max(mt, axis=-1, keepdims=True))
        dt = None
        acc = None
        for s, v in parts:
            p = jnp.exp(s - m)
            for t in tiles(p):
                dt = t if dt is None else dt + t
            pv = _dot(p.astype(BF16), v)
            acc = pv if acc is None else acc + pv
        den = jnp.exp(sink - m) + jnp.sum(dt, axis=-1, keepdims=True)
        return acc * (1.0 / den)

    def lat_block(i, carry):
        r0 = pl.multiple_of(n_ctx + i * QB, QB)
        rp = pl.multiple_of(r0 - QB, QB)
        rn = pl.multiple_of(jnp.minimum(r0 + QB, n_ctx + n_lat - QB), QB)
        q4 = stack_q(q_ref[0, pl.ds(r0, QB), :])
        lo_col = jnp.where(i > 0, rq, LANES)
        hi_col = jnp.where(i < nb - 1, rq, -1)
        sp = jnp.where(col >= lo_col, _dot_nt(q4, k_ref[0, pl.ds(rp, QB), :]), MASK_VALUE)
        ss = _dot_nt(q4, k_ref[0, pl.ds(r0, QB), :])
        sn = jnp.where(col <= hi_col, _dot_nt(q4, k_ref[0, pl.ds(rn, QB), :]), MASK_VALUE)
        sc = _dot_nt(q4, kc)
        o = softmax_pv([(sp, v_ref[0, pl.ds(rp, QB), :]), (ss, v_ref[0, pl.ds(r0, QB), :]),
                        (sn, v_ref[0, pl.ds(rn, QB), :]), (sc, vc)])
        o_ref[0, pl.ds(r0, QB), :] = unstack_o(o).astype(BF16)
        return carry

    lax.fori_loop(0, nb, lat_block, 0, unroll=2)

    for j in range(n_ctx // QB):
        if ctx_out:
            q4 = stack_q(q_ref[0, j * QB:(j + 1) * QB, :])
            o = softmax_pv([(_dot_nt(q4, kc), vc)])
            o_ref[0, j * QB:(j + 1) * QB, :] = unstack_o(o).astype(BF16)
        else:
            o_ref[0, j * QB:(j + 1) * QB, :] = jnp.zeros((QB, 2 * LANES), BF16)


def _attention(sink, q, k, v, n_ctx, ctx_out):
    B, S, _ = q.shape
    return pl.pallas_call(
        functools.partial(_attn_kernel, n_ctx=n_ctx, n_lat=S - n_ctx, ctx_out=ctx_out),
        grid=(B, ATT_KV_HEADS),
        in_specs=[
            pl.BlockSpec(memory_space=pltpu.SMEM),
            pl.BlockSpec((1, S, 2 * LANES), lambda b, h: (b, 0, h)),
            pl.BlockSpec((1, S, LANES), lambda b, h: (b, 0, h)),
            pl.BlockSpec((1, S, LANES), lambda b, h: (b, 0, h)),
        ],
        out_specs=pl.BlockSpec((1, S, 2 * LANES), lambda b, h: (b, 0, h)),
        out_shape=jax.ShapeDtypeStruct((B, S, ATT_Q_W), BF16),
        compiler_params=_cparams(2),
        name="attention",
    )(sink, q, k, v)


def _hgrn_scores(q, b, k, v, d, slot, st_ref, sgn_ref, mask_ref, coef_ref, sc_ref, qh_ref, sti_ref):
    C = CHUNK
    SUB = SUBLANES
    rev = d == 1
    b_p1 = pltpu.roll(b, 1, axis=0)
    b_p2 = pltpu.roll(b, 2, axis=0)
    b_n1 = pltpu.roll(b, C - 1, axis=0)
    b_n2 = pltpu.roll(b, C - 2, axis=0)
    rows_of = lambda a, i, n: a[i * n:(i + 1) * n]
    sc = [None] * (C // SUB)

    def add_rows(first_row, p):
        for i in range(p.shape[0] // SUB):
            j = first_row // SUB + i
            blk = p[i * SUB:(i + 1) * SUB]
            sc[j] = blk if sc[j] is None else sc[j] + blk

    for lv in range(CHUNK_LEVELS):
        half = 1 << lv
        if half < SUB:
            sgn = sgn_ref[d, lv]
            if lv == 0:
                bm = (b_n1 if rev else b_p1) * coef_ref[d, 0] + b * coef_ref[d, 1]
            elif lv == 1:
                far = b_n2 if rev else b_p2
                bm = far * coef_ref[d, 2] + b_p1 * coef_ref[d, 3] + b * coef_ref[d, 4] + b_n1 * coef_ref[d, 5]
            else:
                nblk = C // (2 * half)
                b3 = b.reshape(nblk, 2 * half, HG_DIM)
                ref_row = half if rev else half - 1
                bm = jnp.broadcast_to(b3[:, ref_row:ref_row + 1, :], (nblk, 2 * half, HG_DIM)).reshape(C, HG_DIM)
            x = (jnp.where(sgn > 0, q, k) * jnp.exp2((b - bm) * sgn)).astype(BF16)
            add_rows(0, _dot_nt(x, x) * mask_ref[d, lv])
        else:
            xs, xq, q_first = [], [], []
            for blk in range(C // half):
                is_q = (blk % 2 == 0) if rev else (blk % 2 == 1)
                pair0 = (blk // 2) * 2 * half
                ref = pair0 + (half if rev else half - 1)
                bb = rows_of(b, blk, half)
                bm = b[ref:ref + 1, :]
                xb = ((rows_of(q, blk, half) * jnp.exp2(bb - bm)) if is_q
                      else (rows_of(k, blk, half) * jnp.exp2(bm - bb))).astype(BF16)
                xs.append(xb)
                if is_q:
                    xq.append(xb)
                    q_first.append(blk * half)
            p = _dot_nt(jnp.concatenate(xq, axis=0), jnp.concatenate(xs, axis=0))
            for i, r0 in enumerate(q_first):
                add_rows(r0, p[i * half:(i + 1) * half] * mask_ref[d, lv, r0:r0 + half, :])
    diag = jnp.sum(q * k, axis=-1, keepdims=True)
    scores = jnp.concatenate(sc, axis=0) + diag * mask_ref[d, CHUNK_LEVELS]
    b_end = b[0:1, :] if rev else b[C - 1:C, :]
    st = st_ref[d]
    sc_ref[d, slot] = scores.astype(BF16)
    qh_ref[d, slot] = (q * jnp.exp2(b)).astype(BF16)
    sti_ref[d, slot] = st.astype(BF16)
    kh = (k * jnp.exp2(b_end - b)).astype(BF16)
    st_ref[d] = st * jnp.exp2(b_end) + _dot_tn(v, kh)


def _hgrn_output(v, d, slot, sc_ref, qh_ref, sti_ref):
    return _dot(sc_ref[d, slot], v) + _dot_nt(qh_ref[d, slot], sti_ref[d, slot])


def _hgrn_kernel(q_ref, bf_ref, kf_ref, bb_ref, kb_ref, v_ref, g_ref, ng_ref, sgn_ref, mask_ref,
                 coef_ref, o_ref, of_ref, ob_ref, st_ref, sc_ref, qh_ref, sti_ref, *, n_ctx_chunks, n_lat_chunks):
    nc, nl = n_ctx_chunks, n_lat_chunks
    n = nc + nl
    st_ref[...] = jnp.zeros(st_ref.shape, F32)
    sc_ref[...] = jnp.zeros(sc_ref.shape, BF16)
    qh_ref[...] = jnp.zeros(qh_ref.shape, BF16)
    sti_ref[...] = jnp.zeros(sti_ref.shape, BF16)
    b_refs = (bf_ref, bb_ref)
    k_refs = (kf_ref, kb_ref)
    o_refs = (of_ref, ob_ref)
    consts = (st_ref, sgn_ref, mask_ref, coef_ref, sc_ref, qh_ref, sti_ref)

    def rows_of_step(j, d):
        c = j if d == 0 else jnp.where(j < nc, nc - 1 - j, 2 * nc + nl - 1 - j)
        return pl.ds(pl.multiple_of(c * CHUNK, CHUNK), CHUNK)

    def first_half(j, slot):
        for d in range(2):
            r = rows_of_step(j, d)
            _hgrn_scores(q_ref[0, r, :], b_refs[d][0, r, :], k_refs[d][0, r, :], v_ref[0, r, :], d, slot, *consts)

    def second_half(j, slot):
        for d in range(2):
            r = rows_of_step(j, d)
            o_refs[d][r, :] = _hgrn_output(v_ref[0, r, :], d, slot, sc_ref, qh_ref, sti_ref)

    def trip(i, carry):
        j0 = 2 * i
        second_half(jnp.maximum(j0 - 1, 0), 1)
        first_half(j0, 0)
        second_half(j0, 0)
        first_half(j0 + 1, 1)
        return carry

    lax.fori_loop(0, n // 2, trip, 0)
    second_half(n - 1, 1)

    def readout(j, carry):
        rows = pl.ds(pl.multiple_of(j * CHUNK, CHUNK), CHUNK)
        tot = of_ref[rows, :] + ob_ref[rows, :]
        y = tot * lax.rsqrt(jnp.mean(tot * tot, axis=-1, keepdims=True) + RMS_EPS) * ng_ref[...]
        o_ref[0, rows, :] = (y * _silu(g_ref[0, rows, :])).astype(BF16)
        return carry

    lax.fori_loop(0, n, readout, 0)


def _hgrn(hq, bf, kf, bb, kb, hi, hgate, norm_g, consts, n_ctx):
    B, S, _ = hq.shape
    assert (S // CHUNK) % 2 == 0
    col = pl.BlockSpec((1, S, HG_DIM), lambda b, h: (b, 0, h))
    return pl.pallas_call(
        functools.partial(_hgrn_kernel, n_ctx_chunks=n_ctx // CHUNK, n_lat_chunks=(S - n_ctx) // CHUNK),
        grid=(B, HG_HEADS),
        in_specs=[col, col, col, col, col, col, col,
                  pl.BlockSpec((1, HG_DIM), lambda b, h: (0, 0))] + [_resident(a.shape) for a in consts],
        out_specs=col,
        out_shape=jax.ShapeDtypeStruct((B, S, HG_W), BF16),
        scratch_shapes=[pltpu.VMEM((S, HG_DIM), F32), pltpu.VMEM((S, HG_DIM), F32),
                        pltpu.VMEM((2, HG_DIM, HG_DIM), F32), pltpu.VMEM((2, 2, CHUNK, CHUNK), BF16),
                        pltpu.VMEM((2, 2, CHUNK, HG_DIM), BF16), pltpu.VMEM((2, 2, HG_DIM, HG_DIM), BF16)],
        compiler_params=_cparams(2),
        name="hgrn2",
    )(hq, bf, kf, bb, kb, hi, hgate, norm_g, *consts)


RING = 4
MIXER_FFN_LAG = 2
CONF_FFN_LAG = 4
CONF_ROWS = 128


def _segment_edges(i, n_tiles, n_ctx_tiles):
    is_first = (i == 0) | (i == n_ctx_tiles)
    is_last = (i == n_tiles - 1) | (i == n_ctx_tiles - 1)
    return is_first, is_last


def _ffn_tile(x, xp, xn, mod_ref, is_first, is_last, wup_ref, dww_ref, dwb_ref, wdn_ref, lng_ref, lnb_ref):
    shift = mod_ref[0, 0, 3:4, :]
    scale = mod_ref[0, 0, 4:5, :]
    gate_mod = mod_ref[0, 0, 5:6, :]
    mod = lambda a: a * (1.0 + scale) + shift
    hp = jnp.where(is_first, 0.0, mod(xp))
    hn = jnp.where(is_last, 0.0, mod(xn))
    h_ext = jnp.concatenate([hp, mod(x), hn], axis=0).astype(BF16)
    n_ext = TM + 2 * FFN_HALO
    ug = _dot(h_ext, wup_ref[:, 0:FFN_HIDDEN])
    uv = _dot(mod(x).astype(BF16), wup_ref[:, FFN_HIDDEN:])
    lo = pltpu.roll(ug, 1, axis=0)[FFN_HALO:FFN_HALO + TM]
    hi = pltpu.roll(ug, n_ext - 1, axis=0)[FFN_HALO:FFN_HALO + TM]
    conv = (lo * dww_ref[0:1, :] + ug[FFN_HALO:FFN_HALO + TM] * dww_ref[1:2, :] + hi * dww_ref[2:3, :]
            + dwb_ref[...])
    act = (_silu(conv) * uv).astype(BF16)
    y = _dot(act, wdn_ref[...])
    out = _layer_norm(DEEPNORM_ALPHA * x + gate_mod * y, lng_ref[...], lnb_ref[...])
    marks = ([ug[0:SUBLANES, c:c + LANES] for c in range(0, FFN_HIDDEN, 2 * MXU_N)]
             + [uv[0:SUBLANES, c:c + LANES] for c in range(0, FFN_HIDDEN, 2 * MXU_N)]
             + [y[0:SUBLANES, c:c + LANES] for c in range(0, D_MODEL, MXU_N)])
    return out, marks


def _ring_tiles(s, ring_ref, lag):
    cur = (s + 2 * RING - lag) % RING
    prv = (s + 2 * RING - lag - 1) % RING
    nxt = (s + 2 * RING - lag + 1) % RING
    return ring_ref[cur], ring_ref[prv, TM - FFN_HALO:TM, :], ring_ref[nxt, 0:FFN_HALO, :]


def _lagged_edges(s, lag, n_tiles, n_ctx_tiles, n_total):
    t = jnp.clip(s - lag, 0, n_total - 1)
    return _segment_edges(t % n_tiles, n_tiles, n_ctx_tiles)


def _conf_glu(x, xp, xn, mod_ref, is_first, is_last, w1_ref, b1_ref):
    shift = mod_ref[0, 0, 0:1, :]
    scale = mod_ref[0, 0, 1:2, :]
    n_ext = TM + 2 * CONF_HALO
    x_ext = jnp.concatenate([xp, x, xn], axis=0)
    h_ext = (x_ext * (1.0 + scale) + shift).astype(BF16)
    a = _dot(h_ext, w1_ref[...]) + b1_ref[...]
    a = a[:, :D_MODEL] * _sigmoid(a[:, D_MODEL:])
    rows = lax.broadcasted_iota(jnp.int32, (n_ext, 1), 0)
    pad = ((rows < CONF_HALO) & is_first) | ((rows >= CONF_HALO + TM) & is_last)
    return jnp.where(pad, 0.0, a)


def _conf_dwconv(a_ref, ia, c_ref, ic, dww_ref, dwb_ref, marks, zero_ref):
    half = (CONF_KERNEL - 1) // 2
    base = CONF_HALO - half
    n_slab = CONF_ROWS + 2 * CONF_HALO

    for cb in range(D_MODEL // LANES):
        lanes = slice(cb * LANES, (cb + 1) * LANES)
        for rb in range(TM // CONF_ROWS):
            slab = a_ref[ia, rb * CONF_ROWS:rb * CONF_ROWS + n_slab, lanes]
            mark = marks[(cb * (TM // CONF_ROWS) + rb) % len(marks)]
            edge = pltpu.bitcast(pltpu.bitcast(mark, jnp.int32) & zero_ref[...], F32)
            acc = jnp.broadcast_to(dwb_ref[:, lanes] + edge[0:1, :], (CONF_ROWS, LANES))
            for r in range(SUBLANES):
                sh = pltpu.roll(slab, n_slab - r, axis=0) if r else slab
                for mm in range(-(-(base + CONF_KERNEL) // SUBLANES)):
                    kk = SUBLANES * mm + r - base
                    if 0 <= kk < CONF_KERNEL:
                        acc = acc + sh[SUBLANES * mm:SUBLANES * mm + CONF_ROWS] * dww_ref[kk:kk + 1, lanes]
            c_ref[ic, rb * CONF_ROWS:(rb + 1) * CONF_ROWS, lanes] = acc


def _conf_out(conv, x, mod_ref, cg_ref, cb_ref, w2_ref, b2_ref, lng_ref, lnb_ref):
    gate_mod = mod_ref[0, 0, 2:3, :]
    z = _silu(_layer_norm(conv, cg_ref[...], cb_ref[...])).astype(BF16)
    y = _dot(z, w2_ref[...]) + b2_ref[...]
    return _layer_norm(DEEPNORM_ALPHA * x + gate_mod * y, lng_ref[...], lnb_ref[...])


def _zero_once(s, *refs):
    @pl.when(s == 0)
    def _():
        for ref in refs:
            ref[...] = jnp.zeros(ref.shape, F32)


def _conf_ffn_kernel(x_ref, xp_ref, xn_ref, xres_ref, moda_ref, modc_ref, zero_ref, modf_ref,
                     w1_ref, b1_ref, cdww_ref, cdwb_ref, cg_ref, cb_ref, w2_ref, b2_ref, tlng_ref, tlnb_ref,
                     wup_ref, fdww_ref, fdwb_ref, wdn_ref, flng_ref, flnb_ref,
                     o_ref, ring_ref, a_ref, c_ref, *, n_tiles, n_ctx_tiles, n_total):
    s = pl.program_id(0)
    _zero_once(s, ring_ref, a_ref, c_ref)
    edges = functools.partial(_lagged_edges, s, n_tiles=n_tiles, n_ctx_tiles=n_ctx_tiles, n_total=n_total)
    ffn_in = _ring_tiles(s, ring_ref, CONF_FFN_LAG)
    ring_ref[(s + 2) % RING] = _conf_out(c_ref[s % 2], xres_ref[0], modc_ref, cg_ref, cb_ref, w2_ref, b2_ref,
                                         tlng_ref, tlnb_ref)
    o_ref[0], marks = _ffn_tile(*ffn_in, modf_ref, *edges(CONF_FFN_LAG),
                                wup_ref, fdww_ref, fdwb_ref, wdn_ref, flng_ref, flnb_ref)
    _conf_dwconv(a_ref, (s + 1) % 2, c_ref, (s + 1) % 2, cdww_ref, cdwb_ref, marks, zero_ref)
    a_ref[s % 2] = _conf_glu(x_ref[0], xp_ref[0], xn_ref[0], moda_ref, *edges(0), w1_ref, b1_ref)


def _mixout_ffn_kernel(att_ref, hg_ref, x_ref, modt_ref, modf_ref, wo_ref, tlng_ref, tlnb_ref,
                       wup_ref, fdww_ref, fdwb_ref, wdn_ref, flng_ref, flnb_ref,
                       o_ref, ring_ref, *, n_tiles, n_ctx_tiles, n_total):
    s = pl.program_id(0)
    _zero_once(s, ring_ref)
    ffn_in = _ring_tiles(s, ring_ref, MIXER_FFN_LAG)
    y = _dot(att_ref[0], wo_ref[0:ATT_Q_W, :]) + _dot(hg_ref[0], wo_ref[ATT_Q_W:, :])
    o_ref[0], _ = _ffn_tile(*ffn_in, modf_ref, *_lagged_edges(s, MIXER_FFN_LAG, n_tiles, n_ctx_tiles, n_total),
                            wup_ref, fdww_ref, fdwb_ref, wdn_ref, flng_ref, flnb_ref)
    gate = modt_ref[0, 0, 2:3, :]
    ring_ref[s % RING] = _layer_norm(DEEPNORM_ALPHA * x_ref[0] + gate * y, tlng_ref[...], tlnb_ref[...])


def _tail_ffn(kind, acts, mods, tail_params, ffn_params, n_in_ctx_tiles, skip_ctx):
    B, S, _ = acts[-1].shape
    off = n_in_ctx_tiles if skip_ctx else 0
    nt = S // TM - off
    nct = 0 if skip_ctx else n_in_ctx_tiles
    n_total = B * nt
    ffn_lag = CONF_FFN_LAG if kind == "conformer" else MIXER_FFN_LAG

    def lagged(lag):
        def tile(s):
            t = jnp.clip(s - lag, 0, n_total - 1)
            return t // nt, t % nt
        return tile

    tail_tile = lagged(0)

    def rows(w, tile_fn=tail_tile):
        def index(s):
            b, i = tile_fn(s)
            return (b, i + off, 0)
        return pl.BlockSpec((1, TM, w), index)

    def halo(before):
        per = TM // CONF_HALO

        def index(s):
            b, i = tail_tile(s)
            blk = (i + off) * per - 1 if before else (i + off + 1) * per
            return (b, jnp.clip(blk, 0, (S // TM) * per - 1), 0)
        return pl.BlockSpec((1, CONF_HALO, D_MODEL), index)

    def mod_spec(tile_fn):
        def index(s):
            b, i = tile_fn(s)
            return (b, jnp.where(i < nct, 0, 1), 0, 0)
        return pl.BlockSpec((1, 1, 6, D_MODEL), index)

    def out_index(s):
        b, i = lagged(ffn_lag)(s)
        return (b, i, 0)

    small = lambda a: pl.BlockSpec(a.shape, lambda s: (0,) * a.ndim)
    spec_of = lambda a: _resident(a.shape) if a.size * a.dtype.itemsize > (1 << 20) else small(a)
    scratch = [pltpu.VMEM((RING, TM, D_MODEL), F32)]
    if kind == "conformer":
        xs, = acts
        body = _conf_ffn_kernel
        zero = jnp.zeros((SUBLANES, LANES), jnp.int32)
        in_specs = [rows(D_MODEL), halo(True), halo(False), rows(D_MODEL, lagged(2)),
                    mod_spec(tail_tile), mod_spec(lagged(2)), small(zero)]
        operands = [xs, xs, xs, xs, mods, mods, zero]
        scratch += [pltpu.VMEM((2, TM + 2 * CONF_HALO, D_MODEL), F32), pltpu.VMEM((2, TM, D_MODEL), F32)]
    else:
        body = _mixout_ffn_kernel
        in_specs = [rows(a.shape[2]) for a in acts] + [mod_spec(tail_tile)]
        operands = list(acts) + [mods]
    in_specs += [mod_spec(lagged(ffn_lag))] + [spec_of(a) for a in tail_params + ffn_params]
    operands += [mods] + list(tail_params) + list(ffn_params)
    return pl.pallas_call(
        functools.partial(body, n_tiles=nt, n_ctx_tiles=nct, n_total=n_total),
        grid=(n_total + ffn_lag,),
        in_specs=in_specs,
        out_specs=pl.BlockSpec((1, TM, D_MODEL), out_index),
        out_shape=jax.ShapeDtypeStruct((B, nt * TM, D_MODEL), F32),
        scratch_shapes=scratch,
        compiler_params=_cparams(1),
        name=kind + "_ffn",
    )(*operands)


def _rope_tables(n_ctx, n_lat):
    rows = n_lat // GRID_W
    row = jnp.repeat(jnp.arange(rows), GRID_W).astype(F32)
    col = jnp.tile(jnp.arange(GRID_W), rows).astype(F32)
    quarter = ATT_HEAD_DIM // 4
    inv_freq = ROPE_BASE ** (-jnp.arange(quarter, dtype=F32) / quarter)
    ang_r = row[:, None] * inv_freq
    ang_c = col[:, None] * inv_freq
    ang = jnp.concatenate([ang_r, ang_r, ang_c, ang_c], axis=-1)
    ang = jnp.concatenate([jnp.zeros((n_ctx, ATT_HEAD_DIM), F32), ang], axis=0)
    ang = jnp.concatenate([ang, ang], axis=-1)
    cos, sin = jnp.cos(ang), jnp.sin(ang)
    low = (jnp.arange(LANES) % (2 * quarter)) < quarter
    return cos, jnp.where(low, -sin, 0.0), jnp.where(low, 0.0, sin)


def _chunk_constants():
    t = jnp.arange(CHUNK)
    full = lambda col: jnp.broadcast_to(col[:, None], (CHUNK, HG_DIM)).astype(F32)
    sgn, mask = [], []
    for d in range(2):
        sgn_d, mask_d = [], []
        for lv in range(CHUNK_LEVELS):
            is_q = ((t >> lv) & 1) == (1 - d)
            same = (t[:, None] >> (lv + 1)) == (t[None, :] >> (lv + 1))
            sgn_d.append(full(jnp.where(is_q, 1.0, -1.0)))
            mask_d.append((same & is_q[:, None] & ~is_q[None, :]).astype(F32))
        mask_d.append((t[:, None] == t[None, :]).astype(F32))
        sgn.append(jnp.stack(sgn_d))
        mask.append(jnp.stack(mask_d))
    odd = (t & 1) == 1
    r = t & 3
    coef = jnp.stack([
        jnp.stack([full(odd), full(~odd), full(r == 3), full(r == 2), full(r == 1), full(r == 0)]),
        jnp.stack([full(~odd), full(odd), full(r == 0), full(r == 3), full(r == 2), full(r == 1)]),
    ])
    return jnp.stack(sgn), jnp.stack(mask), coef


def _prefix_matrices():
    t = jnp.arange(CHUNK)
    lower = t[None, :] <= t[:, None]
    return jnp.stack([lower, lower.T]).astype(BF16)


def kernel(x, c, ctx, c_ctx, mod_w, mod_b, post_ln_g, post_ln_b, mix_w_in, mix_w_out, att_sink, hg_lb_logits, hg_norm_g, conf_pw1_w, conf_pw1_b, conf_dw_w, conf_dw_b, conf_ln_g, conf_ln_b, conf_pw2_w, conf_pw2_b, ffn_w_up, ffn_dw_w, ffn_dw_b, ffn_w_down):
    B, T, D = x.shape
    L = ctx.shape[1]
    assert D == D_MODEL and L % TM == 0 and T % TM == 0 and L >= ATT_BLOCK and T % GRID_W == 0
    n_ctx_tiles = L // TM

    n_cond = -(-(B + 1) // SUBLANES) * SUBLANES
    cond = jnp.zeros((n_cond, D), F32).at[:B].set(c).at[B].set(c_ctx)
    mod = _modulation(cond, mod_w.astype(BF16), mod_b[:, None, :])
    mod_x = mod[:, :B].reshape(DEPTH, B, 1, 6, D)
    mod_c = jnp.broadcast_to(mod[:, B].reshape(DEPTH, 1, 1, 6, D), (DEPTH, B, 1, 6, D))
    mods = jnp.concatenate([mod_c, mod_x], axis=2)

    cos, sa, sb = _rope_tables(L, T)
    chunk_consts = _chunk_constants()
    tri = _prefix_matrices()
    vec = lambda a: a.reshape(1, -1)

    xs = jnp.concatenate([ctx, x], axis=1)
    has_ctx = True
    for layer in range(DEPTH):
        m = layer // 2
        need_ctx_out = any(j % 2 == 0 for j in range(layer + 1, DEPTH))
        nct = n_ctx_tiles if has_ctx else 0
        lg, lb = post_ln_g[layer], post_ln_b[layer]
        ffn_params = (ffn_w_up[layer].astype(BF16), ffn_dw_w[layer], vec(ffn_dw_b[layer]),
                      ffn_w_down[layer].astype(BF16), vec(lg[1]), vec(lb[1]))
        if layer % 2 == 0:
            q, k, v, hq, bf, kf, bb, kb, hi, hgate = _mix_in(
                xs, mods[layer], mix_w_in[m].astype(BF16), cos, sa, sb, hg_lb_logits, tri, m, nct)
            att = _attention(att_sink[m], q, k, v, L, need_ctx_out)
            hg = _hgrn(hq, bf, kf, bb, kb, hi, hgate, vec(hg_norm_g[m]), chunk_consts, L)
            tail_params = (mix_w_out[m].astype(BF16), vec(lg[0]), vec(lb[0]))
            xs = _tail_ffn("mixer", (att, hg, xs), mods[layer], tail_params, ffn_params, nct, not need_ctx_out)
        else:
            tail_params = (conf_pw1_w[m].astype(BF16), vec(conf_pw1_b[m]), conf_dw_w[m], vec(conf_dw_b[m]),
                           vec(conf_ln_g[m]), vec(conf_ln_b[m]), conf_pw2_w[m].astype(BF16), vec(conf_pw2_b[m]),
                           vec(lg[0]), vec(lb[0]))
            xs = _tail_ffn("conformer", (xs,), mods[layer], tail_params, ffn_params, nct,
                           has_ctx and not need_ctx_out)
        has_ctx = has_ctx and need_ctx_out
    return xs[:, L:] if has_ctx else xs
```

```python
import functools

import jax
import jax.numpy as jnp
from jax import lax
from jax.experimental import pallas as pl
from jax.experimental.pallas import tpu as pltpu

F32 = jnp.float32
BF16 = jnp.bfloat16

D_MODEL = 1024
DEPTH = 4
GRID_W = 64
ATT_HEADS = 8
ATT_KV_HEADS = 2
ATT_HEAD_DIM = 64
ATT_WINDOW = 128
ATT_BLOCK = 128
ROPE_BASE = 10000.0
HG_HEADS = 4
HG_DIM = 128
CONF_KERNEL = 31
FFN_HIDDEN = 2816
DEEPNORM_ALPHA = (2 * DEPTH) ** 0.25
LN_EPS = 1e-5
RMS_EPS = 1e-6
MASK_VALUE = -1e30
LB_FLOOR = 1e-30

ATT_Q_W = ATT_HEADS * ATT_HEAD_DIM
ATT_KV_W = ATT_KV_HEADS * ATT_HEAD_DIM
HG_W = HG_HEADS * HG_DIM
OFF_AQ = 0
OFF_AK = OFF_AQ + ATT_Q_W
OFF_AV = OFF_AK + ATT_KV_W
OFF_HQ = OFF_AV + ATT_KV_W
OFF_FF = OFF_HQ + HG_W
OFF_FB = OFF_FF + HG_W
OFF_HI = OFF_FB + HG_W
OFF_HGATE = OFF_HI + HG_W
MIX_IN_W = OFF_HGATE + HG_W

LANES = 128
SUBLANES = 8
MXU_N = 256
TM = 256
CHUNK = 128
CHUNK_LEVELS = 7
FFN_HALO = SUBLANES
CONF_HALO = 16
VMEM_LIMIT = 56 * 1024 * 1024


def _cparams(n_grid):
    return pltpu.CompilerParams(
        dimension_semantics=("arbitrary",) * n_grid, vmem_limit_bytes=VMEM_LIMIT)


def _dot(a, b):
    return jnp.dot(a, b, preferred_element_type=F32)


def _dot_nt(a, b):
    return lax.dot_general(a, b, (((1,), (1,)), ((), ())), preferred_element_type=F32)


def _dot_tn(a, b):
    return lax.dot_general(a, b, (((0,), (0,)), ((), ())), preferred_element_type=F32)


def _sigmoid(x):
    e = jnp.exp(-jnp.abs(x))
    r = 1.0 / (1.0 + e)
    return jnp.where(x >= 0, r, e * r)


def _silu(x):
    return x * _sigmoid(x)


def _layer_norm(z, g, b):
    mu = jnp.mean(z, axis=-1, keepdims=True)
    zc = z - mu
    var = jnp.mean(zc * zc, axis=-1, keepdims=True)
    return zc * lax.rsqrt(var + LN_EPS) * g + b


def _resident(shape):
    nd = len(shape)
    return pl.BlockSpec(shape, lambda *_: (0,) * nd, pipeline_mode=pl.Buffered(1))


MOD_TN = 1536


def _mod_kernel(cond_ref, w_ref, b_ref, o_ref):
    s = _silu(cond_ref[...])
    hi = s.astype(BF16)
    lo = (s - hi.astype(F32)).astype(BF16)
    w = w_ref[0]
    o_ref[0] = _dot(hi, w) + _dot(lo, w) + b_ref[0]


def _modulation(cond, mod_w, mod_b):
    R = cond.shape[0]
    n6 = mod_w.shape[2]
    return pl.pallas_call(
        _mod_kernel,
        grid=(DEPTH, n6 // MOD_TN),
        in_specs=[
            pl.BlockSpec((R, D_MODEL), lambda l, j: (0, 0)),
            pl.BlockSpec((1, D_MODEL, MOD_TN), lambda l, j: (l, 0, j)),
            pl.BlockSpec((1, 1, MOD_TN), lambda l, j: (l, 0, j)),
        ],
        out_specs=pl.BlockSpec((1, R, MOD_TN), lambda l, j: (l, 0, j)),
        out_shape=jax.ShapeDtypeStruct((DEPTH, R, n6), F32),
        compiler_params=_cparams(2),
        name="modulation",
    )(cond, mod_w, mod_b)


def _stream_tile(x_refs, i, n_ctx_tiles):
    if len(x_refs) == 1:
        return x_refs[0][0]
    return jnp.where(i < n_ctx_tiles, x_refs[0][0], x_refs[1][0])


def _stream_specs(streams, n_ctx_tiles, tile_fn):
    if len(streams) == 1:
        def index(*g):
            b, i = tile_fn(*g)
            return (b, i, 0)
        return [pl.BlockSpec((1, TM, D_MODEL), index)]

    def ctx_index(*g):
        b, i = tile_fn(*g)
        return (b, jnp.minimum(i, n_ctx_tiles - 1), 0)

    def lat_index(*g):
        b, i = tile_fn(*g)
        return (b, jnp.maximum(i - n_ctx_tiles, 0), 0)
    return [pl.BlockSpec((1, TM, D_MODEL), ctx_index), pl.BlockSpec((1, TM, D_MODEL), lat_index)]


def _mix_in_kernel(*refs, layer_m, n_streams, n_ctx_tiles):
    x_refs, refs = refs[:n_streams], refs[n_streams:]
    (mod_ref, w_ref, cos_ref, sa_ref, sb_ref, lbl_ref, tri_ref,
     q_ref, k_ref, v_ref, hq_ref, bf_ref, kf_ref, bb_ref, kb_ref, hi_ref, hg_ref) = refs
    x = _stream_tile(x_refs, pl.program_id(1), n_ctx_tiles)
    shift = mod_ref[0, 0, 0:1, :]
    scale = mod_ref[0, 0, 1:2, :]
    h = (x * (1.0 + scale) + shift).astype(BF16)

    def proj(off, width):
        return _dot(h, w_ref[:, off:off + width])

    cos = cos_ref[...]
    sa = sa_ref[...]
    sb = sb_ref[...]

    def rope(a, reps):
        w = a.shape[1]
        c = jnp.concatenate([cos] * reps, axis=1) if reps > 1 else cos
        s1 = jnp.concatenate([sa] * reps, axis=1) if reps > 1 else sa
        s2 = jnp.concatenate([sb] * reps, axis=1) if reps > 1 else sb
        up = pltpu.roll(a, w - 16, axis=1)
        dn = pltpu.roll(a, 16, axis=1)
        return a * c + up * s1 + dn * s2

    q = rope(proj(OFF_AQ, ATT_Q_W), ATT_Q_W // LANES) * (ATT_HEAD_DIM ** -0.5)
    q_ref[0] = q.astype(BF16)

    lane = lax.broadcasted_iota(jnp.int32, (TM, LANES), 1)
    first = lane < ATT_HEAD_DIM

    def pair_rep(a):
        sw = pltpu.roll(a, ATT_HEAD_DIM, axis=1)
        return jnp.concatenate([jnp.where(first, a, sw), jnp.where(first, sw, a)], axis=1)

    k_ref[0] = pair_rep(rope(proj(OFF_AK, ATT_KV_W), 1)).astype(BF16)
    v_ref[0] = pair_rep(proj(OFF_AV, ATT_KV_W)).astype(BF16)

    hq_ref[0] = proj(OFF_HQ, HG_W)
    hi_ref[0] = proj(OFF_HI, HG_W).astype(BF16)
    hg_ref[0] = proj(OFF_HGATE, HG_W)

    logits = lbl_ref[...]
    n_mix = logits.shape[0]
    mx = logits[0:1, :]
    for r in range(1, n_mix):
        mx = jnp.maximum(mx, logits[r:r + 1, :])
    ex = [jnp.exp(logits[r:r + 1, :] - mx) for r in range(n_mix)]
    tot = ex[0]
    for r in range(1, n_mix):
        tot = tot + ex[r]
    cum = ex[0] / tot
    p0 = cum
    for r in range(1, layer_m + 1):
        cum = cum + ex[r] / tot
    lb = cum - p0
    lb_floor = jnp.maximum(lb, LB_FLOOR)
    one_m = 1.0 - lb

    def gates(fr, d, b_out, k_out):
        e = jnp.exp(-jnp.abs(fr))
        r = 1.0 / (1.0 + e)
        er = e * r
        pos = fr >= 0
        k_out[0] = one_m * jnp.where(pos, er, r)
        lf = jnp.log2(lb_floor + one_m * jnp.where(pos, r, er))
        tri = tri_ref[d]
        for c in range(TM // CHUNK):
            g = lf[c * CHUNK:(c + 1) * CHUNK]
            g_hi = g.astype(BF16)
            g_lo = (g - g_hi.astype(F32)).astype(BF16)
            b_out[0, c * CHUNK:(c + 1) * CHUNK, :] = _dot(tri, g_hi) + _dot(tri, g_lo)

    gates(proj(OFF_FF, HG_W), 0, bf_ref, kf_ref)
    gates(proj(OFF_FB, HG_W), 1, bb_ref, kb_ref)


def _mix_in(streams, mods, w_in, cos, sa, sb, lb_logits, tri, layer_m, n_ctx_tiles):
    B = streams[0].shape[0]
    S = sum(a.shape[1] for a in streams)
    nt = S // TM
    row = lambda w: pl.BlockSpec((1, TM, w), lambda b, i: (b, i, 0))
    tab = pl.BlockSpec((TM, LANES), lambda b, i: (i, 0))
    sds = lambda w, dt: jax.ShapeDtypeStruct((B, S, w), dt)
    return pl.pallas_call(
        functools.partial(_mix_in_kernel, layer_m=layer_m, n_streams=len(streams), n_ctx_tiles=n_ctx_tiles),
        grid=(B, nt),
        in_specs=_stream_specs(streams, n_ctx_tiles, lambda b, i: (b, i)) + [
            pl.BlockSpec((1, 1, 6, D_MODEL), lambda b, i: (b, jnp.where(i < n_ctx_tiles, 0, 1), 0, 0)),
            _resident((D_MODEL, MIX_IN_W)),
            tab, tab, tab,
            _resident(lb_logits.shape),
            _resident(tri.shape),
        ],
        out_specs=[row(ATT_Q_W), row(2 * LANES), row(2 * LANES), row(HG_W), row(HG_W), row(HG_W),
                   row(HG_W), row(HG_W), row(HG_W), row(HG_W)],
        out_shape=[sds(ATT_Q_W, BF16), sds(2 * LANES, BF16), sds(2 * LANES, BF16), sds(HG_W, F32),
                   sds(HG_W, F32), sds(HG_W, F32), sds(HG_W, F32), sds(HG_W, F32), sds(HG_W, BF16),
                   sds(HG_W, F32)],
        compiler_params=_cparams(2),
        name="mix_in",
    )(*streams, mods, w_in, cos, sa, sb, lb_logits, tri)


GROUP = ATT_HEADS // ATT_KV_HEADS


def _attn_kernel(sink_ref, q_ref, k_ref, v_ref, o_ref, p_ref, rden_ref, *, n_ctx, n_lat, ctx_out):
    hkv = pl.program_id(1)
    QB = ATT_BLOCK
    nb = n_lat // QB
    lane = lax.broadcasted_iota(jnp.int32, (QB, LANES), 1)
    first = lane < ATT_HEAD_DIM
    rows4 = lax.broadcasted_iota(jnp.int32, (GROUP * QB, LANES), 0)
    rq = rows4 & (QB - 1)
    col = lax.broadcasted_iota(jnp.int32, (GROUP * QB, LANES), 1)
    grp = lax.broadcasted_iota(jnp.int32, (GROUP * QB, 1), 0) // QB
    sink = jnp.zeros((GROUP * QB, 1), F32)
    for g in range(GROUP):
        sink = jnp.where(grp == g, sink_ref[hkv * GROUP + g], sink)

    def stack_q(q):
        qa, qb = q[:, :LANES], q[:, LANES:]
        z = jnp.zeros_like(qa)
        return jnp.concatenate([jnp.where(first, qa, z), jnp.where(first, z, qa),
                                jnp.where(first, qb, z), jnp.where(first, z, qb)], axis=0)

    def unstack_o(o):
        return jnp.concatenate([jnp.where(first, o[0:QB], o[QB:2 * QB]),
                                jnp.where(first, o[2 * QB:3 * QB], o[3 * QB:4 * QB])], axis=1)

    kc = k_ref[0, 0:n_ctx, :]
    vc = v_ref[0, 0:n_ctx, :]
    tiles = lambda a: [a[:, c:c + LANES] for c in range(0, a.shape[1], LANES)]

    def softmax_parts(parts):
        mt = None
        for s in parts:
            for t in tiles(s):
                mt = t if mt is None else jnp.maximum(mt, t)
        m = jnp.maximum(sink, jnp.max(mt, axis=-1, keepdims=True))
        dt = None
        ps = []
        for s in parts:
            p = jnp.exp(s - m)
            for t in tiles(p):
                dt = t if dt is None else dt + t
            ps.append(p.astype(BF16))
        den = jnp.exp(sink - m) + jnp.sum(dt, axis=-1, keepdims=True)
        return ps, 1.0 / den

    def block_rows(i):
        r0 = pl.multiple_of(n_ctx + i * QB, QB)
        rp = pl.multiple_of(r0 - QB, QB)
        rn = pl.multiple_of(jnp.minimum(r0 + QB, n_ctx + n_lat - QB), QB)
        return rp, r0, rn

    def qk_scores(i):
        rp, r0, rn = block_rows(i)
        q4 = stack_q(q_ref[0, pl.ds(r0, QB), :])
        lo_col = jnp.where(i > 0, rq, LANES)
        hi_col = jnp.where(i < nb - 1, rq, -1)
        sp = jnp.where(col >= lo_col, _dot_nt(q4, k_ref[0, pl.ds(rp, QB), :]), MASK_VALUE)
        ss = _dot_nt(q4, k_ref[0, pl.ds(r0, QB), :])
        sn = jnp.where(col <= hi_col, _dot_nt(q4, k_ref[0, pl.ds(rn, QB), :]), MASK_VALUE)
        return [sp, ss, sn, _dot_nt(q4, kc)]

    def store_probs(parts, slot):
        ps, rden = softmax_parts(parts)
        p_ref[slot] = jnp.concatenate(ps, axis=1)
        rden_ref[slot] = jnp.broadcast_to(rden, (GROUP * QB, LANES))

    def output_stage(i, slot):
        rp, r0, rn = block_rows(i)
        acc = (_dot(p_ref[slot, :, 0:QB], v_ref[0, pl.ds(rp, QB), :])
               + _dot(p_ref[slot, :, QB:2 * QB], v_ref[0, pl.ds(r0, QB), :])
               + _dot(p_ref[slot, :, 2 * QB:3 * QB], v_ref[0, pl.ds(rn, QB), :])
               + _dot(p_ref[slot, :, 3 * QB:], vc))
        o_ref[0, pl.ds(r0, QB), :] = unstack_o(acc * rden_ref[slot]).astype(BF16)

    p_ref[...] = jnp.zeros(p_ref.shape, BF16)
    rden_ref[...] = jnp.zeros(rden_ref.shape, F32)

    def trip(t, carry):
        i0 = 2 * t
        s_a = qk_scores(i0)
        output_stage(jnp.maximum(i0 - 1, 0), 1)
        s_b = qk_scores(i0 + 1)
        store_probs(s_a, 0)
        output_stage(i0, 0)
        store_probs(s_b, 1)
        return carry

    lax.fori_loop(0, nb // 2, trip, 0)
    output_stage(nb - 1, 1)

    for j in range(n_ctx // QB):
        if ctx_out:
            q4 = stack_q(q_ref[0, j * QB:(j + 1) * QB, :])
            (p,), rden = softmax_parts([_dot_nt(q4, kc)])
            o_ref[0, j * QB:(j + 1) * QB, :] = unstack_o(_dot(p, vc) * rden).astype(BF16)
        else:
            o_ref[0, j * QB:(j + 1) * QB, :] = jnp.zeros((QB, 2 * LANES), BF16)


def _attention(sink, q, k, v, n_ctx, ctx_out):
    B, S, _ = q.shape
    return pl.pallas_call(
        functools.partial(_attn_kernel, n_ctx=n_ctx, n_lat=S - n_ctx, ctx_out=ctx_out),
        grid=(B, ATT_KV_HEADS),
        in_specs=[
            pl.BlockSpec(memory_space=pltpu.SMEM),
            pl.BlockSpec((1, S, 2 * LANES), lambda b, h: (b, 0, h)),
            pl.BlockSpec((1, S, LANES), lambda b, h: (b, 0, h)),
            pl.BlockSpec((1, S, LANES), lambda b, h: (b, 0, h)),
        ],
        out_specs=pl.BlockSpec((1, S, 2 * LANES), lambda b, h: (b, 0, h)),
        out_shape=jax.ShapeDtypeStruct((B, S, ATT_Q_W), BF16),
        scratch_shapes=[pltpu.VMEM((2, GROUP * ATT_BLOCK, 3 * ATT_BLOCK + n_ctx), BF16),
                        pltpu.VMEM((2, GROUP * ATT_BLOCK, LANES), F32)],
        compiler_params=_cparams(2),
        name="attention",
    )(sink, q, k, v)


def _hgrn_scores(q, b, k, v, d, slot, st_ref, sgn_ref, mask_ref, coef_ref, sc_ref, qh_ref, sti_ref):
    C = CHUNK
    SUB = SUBLANES
    rev = d == 1
    b_p1 = pltpu.roll(b, 1, axis=0)
    b_p2 = pltpu.roll(b, 2, axis=0)
    b_n1 = pltpu.roll(b, C - 1, axis=0)
    b_n2 = pltpu.roll(b, C - 2, axis=0)
    rows_of = lambda a, i, n: a[i * n:(i + 1) * n]
    sc = [None] * (C // SUB)

    def add_rows(first_row, p):
        for i in range(p.shape[0] // SUB):
            j = first_row // SUB + i
            blk = p[i * SUB:(i + 1) * SUB]
            sc[j] = blk if sc[j] is None else sc[j] + blk

    for lv in range(CHUNK_LEVELS):
        half = 1 << lv
        if half < SUB:
            sgn = sgn_ref[d, lv]
            if lv == 0:
                bm = (b_n1 if rev else b_p1) * coef_ref[d, 0] + b * coef_ref[d, 1]
            elif lv == 1:
                far = b_n2 if rev else b_p2
                bm = far * coef_ref[d, 2] + b_p1 * coef_ref[d, 3] + b * coef_ref[d, 4] + b_n1 * coef_ref[d, 5]
            else:
                nblk = C // (2 * half)
                b3 = b.reshape(nblk, 2 * half, HG_DIM)
                ref_row = half if rev else half - 1
                bm = jnp.broadcast_to(b3[:, ref_row:ref_row + 1, :], (nblk, 2 * half, HG_DIM)).reshape(C, HG_DIM)
            x = (jnp.where(sgn > 0, q, k) * jnp.exp2((b - bm) * sgn)).astype(BF16)
            add_rows(0, _dot_nt(x, x) * mask_ref[d, lv])
        else:
            xs, xq, q_first = [], [], []
            for blk in range(C // half):
                is_q = (blk % 2 == 0) if rev else (blk % 2 == 1)
                pair0 = (blk // 2) * 2 * half
                ref = pair0 + (half if rev else half - 1)
                bb = rows_of(b, blk, half)
                bm = b[ref:ref + 1, :]
                xb = ((rows_of(q, blk, half) * jnp.exp2(bb - bm)) if is_q
                      else (rows_of(k, blk, half) * jnp.exp2(bm - bb))).astype(BF16)
                xs.append(xb)
                if is_q:
                    xq.append(xb)
                    q_first.append(blk * half)
            p = _dot_nt(jnp.concatenate(xq, axis=0), jnp.concatenate(xs, axis=0))
            for i, r0 in enumerate(q_first):
                add_rows(r0, p[i * half:(i + 1) * half] * mask_ref[d, lv, r0:r0 + half, :])
    diag = jnp.sum(q * k, axis=-1, keepdims=True)
    scores = jnp.concatenate(sc, axis=0) + diag * mask_ref[d, CHUNK_LEVELS]
    b_end = b[0:1, :] if rev else b[C - 1:C, :]
    st = st_ref[d]
    sc_ref[d, slot] = scores.astype(BF16)
    qh_ref[d, slot] = (q * jnp.exp2(b)).astype(BF16)
    sti_ref[d, slot] = st.astype(BF16)
    kh = (k * jnp.exp2(b_end - b)).astype(BF16)
    st_ref[d] = st * jnp.exp2(b_end) + _dot_tn(v, kh)


def _hgrn_output(v, d, slot, sc_ref, qh_ref, sti_ref):
    return _dot(sc_ref[d, slot], v) + _dot_nt(qh_ref[d, slot], sti_ref[d, slot])


def _hgrn_kernel(q_ref, bf_ref, kf_ref, bb_ref, kb_ref, v_ref, g_ref, ng_ref, sgn_ref, mask_ref,
                 coef_ref, o_ref, of_ref, ob_ref, st_ref, sc_ref, qh_ref, sti_ref, *, n_ctx_chunks, n_lat_chunks):
    nc, nl = n_ctx_chunks, n_lat_chunks
    n = nc + nl
    st_ref[...] = jnp.zeros(st_ref.shape, F32)
    sc_ref[...] = jnp.zeros(sc_ref.shape, BF16)
    qh_ref[...] = jnp.zeros(qh_ref.shape, BF16)
    sti_ref[...] = jnp.zeros(sti_ref.shape, BF16)
    b_refs = (bf_ref, bb_ref)
    k_refs = (kf_ref, kb_ref)
    o_refs = (of_ref, ob_ref)
    consts = (st_ref, sgn_ref, mask_ref, coef_ref, sc_ref, qh_ref, sti_ref)

    def rows_of_step(j, d):
        c = j if d == 0 else jnp.where(j < nc, nc - 1 - j, 2 * nc + nl - 1 - j)
        return pl.ds(pl.multiple_of(c * CHUNK, CHUNK), CHUNK)

    def first_half(j, slot):
        for d in range(2):
            r = rows_of_step(j, d)
            _hgrn_scores(q_ref[0, r, :], b_refs[d][0, r, :], k_refs[d][0, r, :], v_ref[0, r, :], d, slot, *consts)

    def second_half(j, slot):
        for d in range(2):
            r = rows_of_step(j, d)
            o_refs[d][r, :] = _hgrn_output(v_ref[0, r, :], d, slot, sc_ref, qh_ref, sti_ref)

    def trip(i, carry):
        j0 = 2 * i
        first_half(j0, 0)
        second_half(jnp.maximum(j0 - 1, 0), 1)
        first_half(j0 + 1, 1)
        second_half(j0, 0)
        return carry

    lax.fori_loop(0, n // 2, trip, 0)
    second_half(n - 1, 1)

    def readout(j, carry):
        rows = pl.ds(pl.multiple_of(j * CHUNK, CHUNK), CHUNK)
        tot = of_ref[rows, :] + ob_ref[rows, :]
        y = tot * lax.rsqrt(jnp.mean(tot * tot, axis=-1, keepdims=True) + RMS_EPS) * ng_ref[...]
        o_ref[0, rows, :] = (y * _silu(g_ref[0, rows, :])).astype(BF16)
        return carry

    lax.fori_loop(0, n, readout, 0)


def _hgrn(hq, bf, kf, bb, kb, hi, hgate, norm_g, consts, n_ctx):
    B, S, _ = hq.shape
    assert (S // CHUNK) % 2 == 0
    col = pl.BlockSpec((1, S, HG_DIM), lambda b, h: (b, 0, h))
    return pl.pallas_call(
        functools.partial(_hgrn_kernel, n_ctx_chunks=n_ctx // CHUNK, n_lat_chunks=(S - n_ctx) // CHUNK),
        grid=(B, HG_HEADS),
        in_specs=[col, col, col, col, col, col, col,
                  pl.BlockSpec((1, HG_DIM), lambda b, h: (0, 0))] + [_resident(a.shape) for a in consts],
        out_specs=col,
        out_shape=jax.ShapeDtypeStruct((B, S, HG_W), BF16),
        scratch_shapes=[pltpu.VMEM((S, HG_DIM), F32), pltpu.VMEM((S, HG_DIM), F32),
                        pltpu.VMEM((2, HG_DIM, HG_DIM), F32), pltpu.VMEM((2, 2, CHUNK, CHUNK), BF16),
                        pltpu.VMEM((2, 2, CHUNK, HG_DIM), BF16), pltpu.VMEM((2, 2, HG_DIM, HG_DIM), BF16)],
        compiler_params=_cparams(2),
        name="hgrn2",
    )(hq, bf, kf, bb, kb, hi, hgate, norm_g, *consts)


RING = 4
MIXER_FFN_LAG = 2
CONF_FFN_LAG = 4
CONF_ROWS = 128


def _segment_edges(i, n_tiles, n_ctx_tiles):
    is_first = (i == 0) | (i == n_ctx_tiles)
    is_last = (i == n_tiles - 1) | (i == n_ctx_tiles - 1)
    return is_first, is_last


def _ffn_tile(x, xp, xn, mod_ref, is_first, is_last, wup_ref, dww_ref, dwb_ref, wdn_ref, lng_ref, lnb_ref):
    shift = mod_ref[0, 0, 3:4, :]
    scale = mod_ref[0, 0, 4:5, :]
    gate_mod = mod_ref[0, 0, 5:6, :]
    mod = lambda a: a * (1.0 + scale) + shift
    hp = jnp.where(is_first, 0.0, mod(xp))
    hn = jnp.where(is_last, 0.0, mod(xn))
    h_ext = jnp.concatenate([hp, mod(x), hn], axis=0).astype(BF16)
    n_ext = TM + 2 * FFN_HALO
    ug = _dot(h_ext, wup_ref[:, 0:FFN_HIDDEN])
    uv = _dot(mod(x).astype(BF16), wup_ref[:, FFN_HIDDEN:])
    lo = pltpu.roll(ug, 1, axis=0)[FFN_HALO:FFN_HALO + TM]
    hi = pltpu.roll(ug, n_ext - 1, axis=0)[FFN_HALO:FFN_HALO + TM]
    conv = (lo * dww_ref[0:1, :] + ug[FFN_HALO:FFN_HALO + TM] * dww_ref[1:2, :] + hi * dww_ref[2:3, :]
            + dwb_ref[...])
    act = (_silu(conv) * uv).astype(BF16)
    y = _dot(act, wdn_ref[...])
    out = _layer_norm(DEEPNORM_ALPHA * x + gate_mod * y, lng_ref[...], lnb_ref[...])
    marks = ([ug[0:SUBLANES, c:c + LANES] for c in range(0, FFN_HIDDEN, MXU_N)]
             + [uv[0:SUBLANES, c:c + LANES] for c in range(0, FFN_HIDDEN, 2 * MXU_N)][:5])
    return out, marks


def _ring_tiles(s, ring_ref, lag):
    cur = (s + 2 * RING - lag) % RING
    prv = (s + 2 * RING - lag - 1) % RING
    nxt = (s + 2 * RING - lag + 1) % RING
    return ring_ref[cur], ring_ref[prv, TM - FFN_HALO:TM, :], ring_ref[nxt, 0:FFN_HALO, :]


def _lagged_edges(s, lag, n_tiles, n_ctx_tiles, n_total):
    t = jnp.clip(s - lag, 0, n_total - 1)
    return _segment_edges(t % n_tiles, n_tiles, n_ctx_tiles)


def _conf_glu(x, xp, xn, mod_ref, is_first, is_last, w1_ref, b1_ref):
    shift = mod_ref[0, 0, 0:1, :]
    scale = mod_ref[0, 0, 1:2, :]
    n_ext = TM + 2 * CONF_HALO
    x_ext = jnp.concatenate([xp, x, xn], axis=0)
    h_ext = (x_ext * (1.0 + scale) + shift).astype(BF16)
    a = _dot(h_ext, w1_ref[...]) + b1_ref[...]
    a = a[:, :D_MODEL] * _sigmoid(a[:, D_MODEL:])
    rows = lax.broadcasted_iota(jnp.int32, (n_ext, 1), 0)
    pad = ((rows < CONF_HALO) & is_first) | ((rows >= CONF_HALO + TM) & is_last)
    return jnp.where(pad, 0.0, a)


def _conf_dwconv(a_ref, ia, c_ref, ic, dww_ref, dwb_ref, marks, zero_ref):
    half = (CONF_KERNEL - 1) // 2
    base = CONF_HALO - half
    n_slab = CONF_ROWS + 2 * CONF_HALO

    for cb in range(D_MODEL // LANES):
        lanes = slice(cb * LANES, (cb + 1) * LANES)
        for rb in range(TM // CONF_ROWS):
            slab = a_ref[ia, rb * CONF_ROWS:rb * CONF_ROWS + n_slab, lanes]
            mark = marks[(cb * (TM // CONF_ROWS) + rb) % len(marks)]
            edge = pltpu.bitcast(pltpu.bitcast(mark, jnp.int32) & zero_ref[...], F32)
            acc = jnp.broadcast_to(dwb_ref[:, lanes] + edge[0:1, :], (CONF_ROWS, LANES))
            for r in range(SUBLANES):
                sh = pltpu.roll(slab, n_slab - r, axis=0) if r else slab
                for mm in range(-(-(base + CONF_KERNEL) // SUBLANES)):
                    kk = SUBLANES * mm + r - base
                    if 0 <= kk < CONF_KERNEL:
                        acc = acc + sh[SUBLANES * mm:SUBLANES * mm + CONF_ROWS] * dww_ref[kk:kk + 1, lanes]
            c_ref[ic, rb * CONF_ROWS:(rb + 1) * CONF_ROWS, lanes] = acc


def _conf_out(conv, x, mod_ref, cg_ref, cb_ref, w2_ref, b2_ref, lng_ref, lnb_ref):
    gate_mod = mod_ref[0, 0, 2:3, :]
    z = _silu(_layer_norm(conv, cg_ref[...], cb_ref[...])).astype(BF16)
    y = _dot(z, w2_ref[...]) + b2_ref[...]
    return _layer_norm(DEEPNORM_ALPHA * x + gate_mod * y, lng_ref[...], lnb_ref[...])


def _zero_once(s, *refs):
    @pl.when(s == 0)
    def _():
        for ref in refs:
            ref[...] = jnp.zeros(ref.shape, F32)


def _conf_ffn_kernel(x_ref, xp_ref, xn_ref, xres_ref, moda_ref, modc_ref, zero_ref, modf_ref,
                     w1_ref, b1_ref, cdww_ref, cdwb_ref, cg_ref, cb_ref, w2_ref, b2_ref, tlng_ref, tlnb_ref,
                     wup_ref, fdww_ref, fdwb_ref, wdn_ref, flng_ref, flnb_ref,
                     o_ref, ring_ref, a_ref, c_ref, *, n_tiles, n_ctx_tiles, n_total):
    s = pl.program_id(0)
    _zero_once(s, ring_ref, a_ref, c_ref)
    edges = functools.partial(_lagged_edges, s, n_tiles=n_tiles, n_ctx_tiles=n_ctx_tiles, n_total=n_total)
    ffn_in = _ring_tiles(s, ring_ref, CONF_FFN_LAG)
    ring_ref[(s + 2) % RING] = _conf_out(c_ref[s % 2], xres_ref[0], modc_ref, cg_ref, cb_ref, w2_ref, b2_ref,
                                         tlng_ref, tlnb_ref)
    o_ref[0], marks = _ffn_tile(*ffn_in, modf_ref, *edges(CONF_FFN_LAG),
                                wup_ref, fdww_ref, fdwb_ref, wdn_ref, flng_ref, flnb_ref)
    _conf_dwconv(a_ref, (s + 1) % 2, c_ref, (s + 1) % 2, cdww_ref, cdwb_ref, marks, zero_ref)
    a_ref[s % 2] = _conf_glu(x_ref[0], xp_ref[0], xn_ref[0], moda_ref, *edges(0), w1_ref, b1_ref)


def _mixout_ffn_kernel(att_ref, hg_ref, *refs, n_streams, n_tiles, n_ctx_tiles, n_total):
    x_refs, refs = refs[:n_streams], refs[n_streams:]
    (modt_ref, modf_ref, wo_ref, tlng_ref, tlnb_ref, wup_ref, fdww_ref, fdwb_ref, wdn_ref, flng_ref, flnb_ref,
     o_ref, ring_ref) = refs
    s = pl.program_id(0)
    _zero_once(s, ring_ref)
    ffn_in = _ring_tiles(s, ring_ref, MIXER_FFN_LAG)
    y = _dot(att_ref[0], wo_ref[0:ATT_Q_W, :]) + _dot(hg_ref[0], wo_ref[ATT_Q_W:, :])
    o_ref[0], _ = _ffn_tile(*ffn_in, modf_ref, *_lagged_edges(s, MIXER_FFN_LAG, n_tiles, n_ctx_tiles, n_total),
                            wup_ref, fdww_ref, fdwb_ref, wdn_ref, flng_ref, flnb_ref)
    gate = modt_ref[0, 0, 2:3, :]
    x = _stream_tile(x_refs, jnp.minimum(s, n_total - 1) % n_tiles, n_ctx_tiles)
    ring_ref[s % RING] = _layer_norm(DEEPNORM_ALPHA * x + gate * y, tlng_ref[...], tlnb_ref[...])


def _tail_ffn(kind, acts, mods, tail_params, ffn_params, n_in_ctx_tiles, skip_ctx):
    B, S, _ = acts[0].shape
    off = n_in_ctx_tiles if skip_ctx else 0
    nt = S // TM - off
    nct = 0 if skip_ctx else n_in_ctx_tiles
    n_total = B * nt
    ffn_lag = CONF_FFN_LAG if kind == "conformer" else MIXER_FFN_LAG

    def lagged(lag):
        def tile(s):
            t = jnp.clip(s - lag, 0, n_total - 1)
            return t // nt, t % nt
        return tile

    tail_tile = lagged(0)

    def rows(w, tile_fn=tail_tile):
        def index(s):
            b, i = tile_fn(s)
            return (b, i + off, 0)
        return pl.BlockSpec((1, TM, w), index)

    def halo(before):
        per = TM // CONF_HALO

        def index(s):
            b, i = tail_tile(s)
            blk = (i + off) * per - 1 if before else (i + off + 1) * per
            return (b, jnp.clip(blk, 0, (S // TM) * per - 1), 0)
        return pl.BlockSpec((1, CONF_HALO, D_MODEL), index)

    def mod_spec(tile_fn):
        def index(s):
            b, i = tile_fn(s)
            return (b, jnp.where(i < nct, 0, 1), 0, 0)
        return pl.BlockSpec((1, 1, 6, D_MODEL), index)

    def out_index(s):
        b, i = lagged(ffn_lag)(s)
        return (b, i, 0)

    small = lambda a: pl.BlockSpec(a.shape, lambda s: (0,) * a.ndim)
    spec_of = lambda a: _resident(a.shape) if a.size * a.dtype.itemsize > (1 << 20) else small(a)
    scratch = [pltpu.VMEM((RING, TM, D_MODEL), F32)]
    if kind == "conformer":
        xs, = acts
        body = _conf_ffn_kernel
        zero = jnp.zeros((SUBLANES, LANES), jnp.int32)
        in_specs = [rows(D_MODEL), halo(True), halo(False), rows(D_MODEL, lagged(2)),
                    mod_spec(tail_tile), mod_spec(lagged(2)), small(zero)]
        operands = [xs, xs, xs, xs, mods, mods, zero]
        scratch += [pltpu.VMEM((2, TM + 2 * CONF_HALO, D_MODEL), F32), pltpu.VMEM((2, TM, D_MODEL), F32)]
    else:
        streams = acts[2:]
        assert len(streams) == 1 or not skip_ctx
        body = functools.partial(_mixout_ffn_kernel, n_streams=len(streams))
        in_specs = [rows(a.shape[2]) for a in acts[:2]]
        in_specs += [rows(D_MODEL)] if len(streams) == 1 else _stream_specs(streams, nct, tail_tile)
        in_specs += [mod_spec(tail_tile)]
        operands = list(acts) + [mods]
    in_specs += [mod_spec(lagged(ffn_lag))] + [spec_of(a) for a in tail_params + ffn_params]
    operands += [mods] + list(tail_params) + list(ffn_params)
    return pl.pallas_call(
        functools.partial(body, n_tiles=nt, n_ctx_tiles=nct, n_total=n_total),
        grid=(n_total + ffn_lag,),
        in_specs=in_specs,
        out_specs=pl.BlockSpec((1, TM, D_MODEL), out_index),
        out_shape=jax.ShapeDtypeStruct((B, nt * TM, D_MODEL), F32),
        scratch_shapes=scratch,
        compiler_params=_cparams(1),
        name=kind + "_ffn",
    )(*operands)


def _rope_tables(n_ctx, n_lat):
    rows = n_lat // GRID_W
    row = jnp.repeat(jnp.arange(rows), GRID_W).astype(F32)
    col = jnp.tile(jnp.arange(GRID_W), rows).astype(F32)
    quarter = ATT_HEAD_DIM // 4
    inv_freq = ROPE_BASE ** (-jnp.arange(quarter, dtype=F32) / quarter)
    ang_r = row[:, None] * inv_freq
    ang_c = col[:, None] * inv_freq
    ang = jnp.concatenate([ang_r, ang_r, ang_c, ang_c], axis=-1)
    ang = jnp.concatenate([jnp.zeros((n_ctx, ATT_HEAD_DIM), F32), ang], axis=0)
    ang = jnp.concatenate([ang, ang], axis=-1)
    cos, sin = jnp.cos(ang), jnp.sin(ang)
    low = (jnp.arange(LANES) % (2 * quarter)) < quarter
    return cos, jnp.where(low, -sin, 0.0), jnp.where(low, 0.0, sin)


def _chunk_constants():
    t = jnp.arange(CHUNK)
    full = lambda col: jnp.broadcast_to(col[:, None], (CHUNK, HG_DIM)).astype(F32)
    sgn, mask = [], []
    for d in range(2):
        sgn_d, mask_d = [], []
        for lv in range(CHUNK_LEVELS):
            is_q = ((t >> lv) & 1) == (1 - d)
            same = (t[:, None] >> (lv + 1)) == (t[None, :] >> (lv + 1))
            sgn_d.append(full(jnp.where(is_q, 1.0, -1.0)))
            mask_d.append((same & is_q[:, None] & ~is_q[None, :]).astype(F32))
        mask_d.append((t[:, None] == t[None, :]).astype(F32))
        sgn.append(jnp.stack(sgn_d))
        mask.append(jnp.stack(mask_d))
    odd = (t & 1) == 1
    r = t & 3
    coef = jnp.stack([
        jnp.stack([full(odd), full(~odd), full(r == 3), full(r == 2), full(r == 1), full(r == 0)]),
        jnp.stack([full(~odd), full(odd), full(r == 0), full(r == 3), full(r == 2), full(r == 1)]),
    ])
    return jnp.stack(sgn), jnp.stack(mask), coef


def _prefix_matrices():
    t = jnp.arange(CHUNK)
    lower = t[None, :] <= t[:, None]
    return jnp.stack([lower, lower.T]).astype(BF16)


def kernel(x, c, ctx, c_ctx, mod_w, mod_b, post_ln_g, post_ln_b, mix_w_in, mix_w_out, att_sink, hg_lb_logits, hg_norm_g, conf_pw1_w, conf_pw1_b, conf_dw_w, conf_dw_b, conf_ln_g, conf_ln_b, conf_pw2_w, conf_pw2_b, ffn_w_up, ffn_dw_w, ffn_dw_b, ffn_w_down):
    B, T, D = x.shape
    L = ctx.shape[1]
    assert D == D_MODEL and L % TM == 0 and T % TM == 0 and L >= ATT_BLOCK and T % GRID_W == 0
    n_ctx_tiles = L // TM

    n_cond = -(-(B + 1) // SUBLANES) * SUBLANES
    cond = jnp.zeros((n_cond, D), F32).at[:B].set(c).at[B].set(c_ctx)
    mod = _modulation(cond, mod_w.astype(BF16), mod_b[:, None, :])
    mod_x = mod[:, :B].reshape(DEPTH, B, 1, 6, D)
    mod_c = jnp.broadcast_to(mod[:, B].reshape(DEPTH, 1, 1, 6, D), (DEPTH, B, 1, 6, D))
    mods = jnp.concatenate([mod_c, mod_x], axis=2)

    cos, sa, sb = _rope_tables(L, T)
    chunk_consts = _chunk_constants()
    tri = _prefix_matrices()
    vec = lambda a: a.reshape(1, -1)

    streams = (ctx, x)
    has_ctx = True
    for layer in range(DEPTH):
        m = layer // 2
        need_ctx_out = any(j % 2 == 0 for j in range(layer + 1, DEPTH))
        nct = n_ctx_tiles if has_ctx else 0
        lg, lb = post_ln_g[layer], post_ln_b[layer]
        ffn_params = (ffn_w_up[layer].astype(BF16), ffn_dw_w[layer], vec(ffn_dw_b[layer]),
                      ffn_w_down[layer].astype(BF16), vec(lg[1]), vec(lb[1]))
        if layer % 2 == 0:
            q, k, v, hq, bf, kf, bb, kb, hi, hgate = _mix_in(
                streams, mods[layer], mix_w_in[m].astype(BF16), cos, sa, sb, hg_lb_logits, tri, m, nct)
            att = _attention(att_sink[m], q, k, v, L, need_ctx_out)
            hg = _hgrn(hq, bf, kf, bb, kb, hi, hgate, vec(hg_norm_g[m]), chunk_consts, L)
            tail_params = (mix_w_out[m].astype(BF16), vec(lg[0]), vec(lb[0]))
            xs = _tail_ffn("mixer", (att, hg) + streams, mods[layer], tail_params, ffn_params, nct,
                           not need_ctx_out)
        else:
            tail_params = (conf_pw1_w[m].astype(BF16), vec(conf_pw1_b[m]), conf_dw_w[m], vec(conf_dw_b[m]),
                           vec(conf_ln_g[m]), vec(conf_ln_b[m]), conf_pw2_w[m].astype(BF16), vec(conf_pw2_b[m]),
                           vec(lg[0]), vec(lb[0]))
            xs = _tail_ffn("conformer", streams, mods[layer], tail_params, ffn_params, nct,
                           has_ctx and not need_ctx_out)
        streams = (xs,)
        has_ctx = has_ctx and need_ctx_out
    return xs[:, L:] if has_ctx else xs
```

```python
import functools

import jax
import jax.numpy as jnp
from jax import lax
from jax.experimental import pallas as pl
from jax.experimental.pallas import tpu as pltpu

F32 = jnp.float32
BF16 = jnp.bfloat16

D_MODEL = 1024
DEPTH = 4
GRID_W = 64
ATT_HEADS = 8
ATT_KV_HEADS = 2
ATT_HEAD_DIM = 64
ATT_WINDOW = 128
ATT_BLOCK = 128
ROPE_BASE = 10000.0
HG_HEADS = 4
HG_DIM = 128
CONF_KERNEL = 31
FFN_HIDDEN = 2816
DEEPNORM_ALPHA = (2 * DEPTH) ** 0.25
LN_EPS = 1e-5
RMS_EPS = 1e-6
MASK_VALUE = -1e30
LB_FLOOR = 1e-30

ATT_Q_W = ATT_HEADS * ATT_HEAD_DIM
ATT_KV_W = ATT_KV_HEADS * ATT_HEAD_DIM
HG_W = HG_HEADS * HG_DIM
OFF_AQ = 0
OFF_AK = OFF_AQ + ATT_Q_W
OFF_AV = OFF_AK + ATT_KV_W
OFF_HQ = OFF_AV + ATT_KV_W
OFF_FF = OFF_HQ + HG_W
OFF_FB = OFF_FF + HG_W
OFF_HI = OFF_FB + HG_W
OFF_HGATE = OFF_HI + HG_W
MIX_IN_W = OFF_HGATE + HG_W

LANES = 128
SUBLANES = 8
MXU_N = 256
TM = 256
CHUNK = 128
CHUNK_LEVELS = 7
FFN_HALO = SUBLANES
CONF_HALO = 16
VMEM_LIMIT = 56 * 1024 * 1024


def _cparams(n_grid):
    return pltpu.CompilerParams(
        dimension_semantics=("arbitrary",) * n_grid, vmem_limit_bytes=VMEM_LIMIT)


def _dot(a, b):
    return jnp.dot(a, b, preferred_element_type=F32)


def _dot_nt(a, b):
    return lax.dot_general(a, b, (((1,), (1,)), ((), ())), preferred_element_type=F32)


def _dot_tn(a, b):
    return lax.dot_general(a, b, (((0,), (0,)), ((), ())), preferred_element_type=F32)


def _sigmoid(x):
    return 0.5 * jnp.tanh(0.5 * x) + 0.5


def _silu(x):
    h = 0.5 * x
    return h * (1.0 + jnp.tanh(h))


def _layer_norm(z, g, b):
    mu = jnp.mean(z, axis=-1, keepdims=True)
    zc = z - mu
    var = jnp.mean(zc * zc, axis=-1, keepdims=True)
    return zc * lax.rsqrt(var + LN_EPS) * g + b


def _resident(shape):
    nd = len(shape)
    return pl.BlockSpec(shape, lambda *_: (0,) * nd, pipeline_mode=pl.Buffered(1))


MOD_TN = 1536


def _mod_kernel(cond_ref, w_ref, b_ref, o_ref):
    s = _silu(cond_ref[...])
    hi = s.astype(BF16)
    lo = (s - hi.astype(F32)).astype(BF16)
    w = w_ref[0]
    o_ref[0] = _dot(hi, w) + _dot(lo, w) + b_ref[0]


def _modulation(cond, mod_w, mod_b):
    R = cond.shape[0]
    n6 = mod_w.shape[2]
    return pl.pallas_call(
        _mod_kernel,
        grid=(DEPTH, n6 // MOD_TN),
        in_specs=[
            pl.BlockSpec((R, D_MODEL), lambda l, j: (0, 0)),
            pl.BlockSpec((1, D_MODEL, MOD_TN), lambda l, j: (l, 0, j)),
            pl.BlockSpec((1, 1, MOD_TN), lambda l, j: (l, 0, j)),
        ],
        out_specs=pl.BlockSpec((1, R, MOD_TN), lambda l, j: (l, 0, j)),
        out_shape=jax.ShapeDtypeStruct((DEPTH, R, n6), F32),
        compiler_params=_cparams(2),
        name="modulation",
    )(cond, mod_w, mod_b)


def _stream_tile(x_refs, i, n_ctx_tiles):
    if len(x_refs) == 1:
        return x_refs[0][0]
    return jnp.where(i < n_ctx_tiles, x_refs[0][0], x_refs[1][0])


def _stream_specs(streams, n_ctx_tiles, tile_fn):
    if len(streams) == 1:
        def index(*g):
            b, i = tile_fn(*g)
            return (b, i, 0)
        return [pl.BlockSpec((1, TM, D_MODEL), index)]

    def ctx_index(*g):
        b, i = tile_fn(*g)
        return (b, jnp.minimum(i, n_ctx_tiles - 1), 0)

    def lat_index(*g):
        b, i = tile_fn(*g)
        return (b, jnp.maximum(i - n_ctx_tiles, 0), 0)
    return [pl.BlockSpec((1, TM, D_MODEL), ctx_index), pl.BlockSpec((1, TM, D_MODEL), lat_index)]


def _mix_in_kernel(*refs, layer_m, n_streams, n_ctx_tiles):
    x_refs, refs = refs[:n_streams], refs[n_streams:]
    (mod_ref, w_ref, cos_ref, sa_ref, sb_ref, lbl_ref, tri_ref,
     q_ref, k_ref, v_ref, hq_ref, bf_ref, kf_ref, bb_ref, kb_ref, hi_ref, hg_ref) = refs
    x = _stream_tile(x_refs, pl.program_id(1), n_ctx_tiles)
    shift = mod_ref[0, 0, 0:1, :]
    scale = mod_ref[0, 0, 1:2, :]
    h = (x * (1.0 + scale) + shift).astype(BF16)

    def proj(off, width):
        return _dot(h, w_ref[:, off:off + width])

    fr_fwd = proj(OFF_FF, HG_W)
    fr_bwd = proj(OFF_FB, HG_W)

    cos = cos_ref[...]
    sa = sa_ref[...]
    sb = sb_ref[...]

    def rope(a, reps):
        w = a.shape[1]
        c = jnp.concatenate([cos] * reps, axis=1) if reps > 1 else cos
        s1 = jnp.concatenate([sa] * reps, axis=1) if reps > 1 else sa
        s2 = jnp.concatenate([sb] * reps, axis=1) if reps > 1 else sb
        up = pltpu.roll(a, w - 16, axis=1)
        dn = pltpu.roll(a, 16, axis=1)
        return a * c + up * s1 + dn * s2

    q = rope(proj(OFF_AQ, ATT_Q_W), ATT_Q_W // LANES) * (ATT_HEAD_DIM ** -0.5)
    q_ref[0] = q.astype(BF16)

    lane = lax.broadcasted_iota(jnp.int32, (TM, LANES), 1)
    first = lane < ATT_HEAD_DIM

    def pair_rep(a):
        sw = pltpu.roll(a, ATT_HEAD_DIM, axis=1)
        return jnp.concatenate([jnp.where(first, a, sw), jnp.where(first, sw, a)], axis=1)

    k_ref[0] = pair_rep(rope(proj(OFF_AK, ATT_KV_W), 1)).astype(BF16)
    v_ref[0] = pair_rep(proj(OFF_AV, ATT_KV_W)).astype(BF16)

    logits = lbl_ref[...]
    n_mix = logits.shape[0]
    mx = logits[0:1, :]
    for r in range(1, n_mix):
        mx = jnp.maximum(mx, logits[r:r + 1, :])
    ex = [jnp.exp(logits[r:r + 1, :] - mx) for r in range(n_mix)]
    tot = ex[0]
    for r in range(1, n_mix):
        tot = tot + ex[r]
    cum = ex[0] / tot
    p0 = cum
    for r in range(1, layer_m + 1):
        cum = cum + ex[r] / tot
    lb = cum - p0
    lb_floor = jnp.maximum(lb, LB_FLOOR)
    one_m = 1.0 - lb

    def gates(fr, d, b_out, k_out):
        e = jnp.exp(-jnp.abs(fr))
        r = 1.0 / (1.0 + e)
        er = e * r
        pos = fr >= 0
        k_out[0] = one_m * jnp.where(pos, er, r)
        lf = jnp.log2(lb_floor + one_m * jnp.where(pos, r, er))
        tri = tri_ref[d]
        for c in range(TM // CHUNK):
            g = lf[c * CHUNK:(c + 1) * CHUNK]
            g_hi = g.astype(BF16)
            g_lo = (g - g_hi.astype(F32)).astype(BF16)
            b_out[0, c * CHUNK:(c + 1) * CHUNK, :] = _dot(tri, g_hi) + _dot(tri, g_lo)

    hq_ref[0] = proj(OFF_HQ, HG_W)
    gates(fr_fwd, 0, bf_ref, kf_ref)
    hi_ref[0] = proj(OFF_HI, HG_W).astype(BF16)
    gates(fr_bwd, 1, bb_ref, kb_ref)
    hg_ref[0] = proj(OFF_HGATE, HG_W)


def _mix_in(streams, mods, w_in, cos, sa, sb, lb_logits, tri, layer_m, n_ctx_tiles):
    B = streams[0].shape[0]
    S = sum(a.shape[1] for a in streams)
    nt = S // TM
    row = lambda w: pl.BlockSpec((1, TM, w), lambda b, i: (b, i, 0))
    tab = pl.BlockSpec((TM, LANES), lambda b, i: (i, 0))
    sds = lambda w, dt: jax.ShapeDtypeStruct((B, S, w), dt)
    return pl.pallas_call(
        functools.partial(_mix_in_kernel, layer_m=layer_m, n_streams=len(streams), n_ctx_tiles=n_ctx_tiles),
        grid=(B, nt),
        in_specs=_stream_specs(streams, n_ctx_tiles, lambda b, i: (b, i)) + [
            pl.BlockSpec((1, 1, 6, D_MODEL), lambda b, i: (b, jnp.where(i < n_ctx_tiles, 0, 1), 0, 0)),
            _resident((D_MODEL, MIX_IN_W)),
            tab, tab, tab,
            _resident(lb_logits.shape),
            _resident(tri.shape),
        ],
        out_specs=[row(ATT_Q_W), row(2 * LANES), row(2 * LANES), row(HG_W), row(HG_W), row(HG_W),
                   row(HG_W), row(HG_W), row(HG_W), row(HG_W)],
        out_shape=[sds(ATT_Q_W, BF16), sds(2 * LANES, BF16), sds(2 * LANES, BF16), sds(HG_W, F32),
                   sds(HG_W, F32), sds(HG_W, F32), sds(HG_W, F32), sds(HG_W, F32), sds(HG_W, BF16),
                   sds(HG_W, F32)],
        compiler_params=_cparams(2),
        name="mix_in",
    )(*streams, mods, w_in, cos, sa, sb, lb_logits, tri)


GROUP = ATT_HEADS // ATT_KV_HEADS


def _attn_kernel(sink_ref, q_ref, k_ref, v_ref, o_ref, p_ref, rden_ref, *, n_ctx, n_lat, ctx_out):
    hkv = pl.program_id(1)
    QB = ATT_BLOCK
    nb = n_lat // QB
    lane = lax.broadcasted_iota(jnp.int32, (QB, LANES), 1)
    first = lane < ATT_HEAD_DIM
    rows4 = lax.broadcasted_iota(jnp.int32, (GROUP * QB, LANES), 0)
    rq = rows4 & (QB - 1)
    col = lax.broadcasted_iota(jnp.int32, (GROUP * QB, LANES), 1)
    grp = lax.broadcasted_iota(jnp.int32, (GROUP * QB, 1), 0) // QB
    sink = jnp.zeros((GROUP * QB, 1), F32)
    for g in range(GROUP):
        sink = jnp.where(grp == g, sink_ref[hkv * GROUP + g], sink)

    def stack_q(q):
        qa, qb = q[:, :LANES], q[:, LANES:]
        z = jnp.zeros_like(qa)
        return jnp.concatenate([jnp.where(first, qa, z), jnp.where(first, z, qa),
                                jnp.where(first, qb, z), jnp.where(first, z, qb)], axis=0)

    def unstack_o(o):
        return jnp.concatenate([jnp.where(first, o[0:QB], o[QB:2 * QB]),
                                jnp.where(first, o[2 * QB:3 * QB], o[3 * QB:4 * QB])], axis=1)

    kc = k_ref[0, 0:n_ctx, :]
    vc = v_ref[0, 0:n_ctx, :]
    tiles = lambda a: [a[:, c:c + LANES] for c in range(0, a.shape[1], LANES)]

    def softmax_parts(parts):
        mt = None
        for s in parts:
            for t in tiles(s):
                mt = t if mt is None else jnp.maximum(mt, t)
        m = jnp.maximum(sink, jnp.max(mt, axis=-1, keepdims=True))
        dt = None
        ps = []
        for s in parts:
            p = jnp.exp(s - m)
            for t in tiles(p):
                dt = t if dt is None else dt + t
            ps.append(p.astype(BF16))
        den = jnp.exp(sink - m) + jnp.sum(dt, axis=-1, keepdims=True)
        return ps, 1.0 / den

    def block_rows(i):
        r0 = pl.multiple_of(n_ctx + i * QB, QB)
        rp = pl.multiple_of(r0 - QB, QB)
        rn = pl.multiple_of(jnp.minimum(r0 + QB, n_ctx + n_lat - QB), QB)
        return rp, r0, rn

    def qk_scores(i):
        rp, r0, rn = block_rows(i)
        q4 = stack_q(q_ref[0, pl.ds(r0, QB), :])
        lo_col = jnp.where(i > 0, rq, LANES)
        hi_col = jnp.where(i < nb - 1, rq, -1)
        sp = jnp.where(col >= lo_col, _dot_nt(q4, k_ref[0, pl.ds(rp, QB), :]), MASK_VALUE)
        ss = _dot_nt(q4, k_ref[0, pl.ds(r0, QB), :])
        sn = jnp.where(col <= hi_col, _dot_nt(q4, k_ref[0, pl.ds(rn, QB), :]), MASK_VALUE)
        return [sp, ss, sn, _dot_nt(q4, kc)]

    def store_probs(parts, slot):
        ps, rden = softmax_parts(parts)
        p_ref[slot] = jnp.concatenate(ps, axis=1)
        rden_ref[slot] = jnp.broadcast_to(rden, (GROUP * QB, LANES))

    def output_stage(i, slot):
        rp, r0, rn = block_rows(i)
        acc = (_dot(p_ref[slot, :, 0:QB], v_ref[0, pl.ds(rp, QB), :])
               + _dot(p_ref[slot, :, QB:2 * QB], v_ref[0, pl.ds(r0, QB), :])
               + _dot(p_ref[slot, :, 2 * QB:3 * QB], v_ref[0, pl.ds(rn, QB), :])
               + _dot(p_ref[slot, :, 3 * QB:], vc))
        o_ref[0, pl.ds(r0, QB), :] = unstack_o(acc * rden_ref[slot]).astype(BF16)

    p_ref[...] = jnp.zeros(p_ref.shape, BF16)
    rden_ref[...] = jnp.zeros(rden_ref.shape, F32)

    def trip(t, carry):
        i0 = 2 * t
        s_a = qk_scores(i0)
        output_stage(jnp.maximum(i0 - 1, 0), 1)
        s_b = qk_scores(i0 + 1)
        store_probs(s_a, 0)
        output_stage(i0, 0)
        store_probs(s_b, 1)
        return carry

    lax.fori_loop(0, nb // 2, trip, 0)
    output_stage(nb - 1, 1)

    for j in range(n_ctx // QB):
        if ctx_out:
            q4 = stack_q(q_ref[0, j * QB:(j + 1) * QB, :])
            (p,), rden = softmax_parts([_dot_nt(q4, kc)])
            o_ref[0, j * QB:(j + 1) * QB, :] = unstack_o(_dot(p, vc) * rden).astype(BF16)
        else:
            o_ref[0, j * QB:(j + 1) * QB, :] = jnp.zeros((QB, 2 * LANES), BF16)


def _attention(sink, q, k, v, n_ctx, ctx_out):
    B, S, _ = q.shape
    return pl.pallas_call(
        functools.partial(_attn_kernel, n_ctx=n_ctx, n_lat=S - n_ctx, ctx_out=ctx_out),
        grid=(B, ATT_KV_HEADS),
        in_specs=[
            pl.BlockSpec(memory_space=pltpu.SMEM),
            pl.BlockSpec((1, S, 2 * LANES), lambda b, h: (b, 0, h)),
            pl.BlockSpec((1, S, LANES), lambda b, h: (b, 0, h)),
            pl.BlockSpec((1, S, LANES), lambda b, h: (b, 0, h)),
        ],
        out_specs=pl.BlockSpec((1, S, 2 * LANES), lambda b, h: (b, 0, h)),
        out_shape=jax.ShapeDtypeStruct((B, S, ATT_Q_W), BF16),
        scratch_shapes=[pltpu.VMEM((2, GROUP * ATT_BLOCK, 3 * ATT_BLOCK + n_ctx), BF16),
                        pltpu.VMEM((2, GROUP * ATT_BLOCK, LANES), F32)],
        compiler_params=_cparams(2),
        name="attention",
    )(sink, q, k, v)


def _hgrn_scores(q, b, k, v, d, slot, st_ref, sgn_ref, mask_ref, coef_ref, sc_ref, qh_ref, sti_ref):
    C = CHUNK
    SUB = SUBLANES
    rev = d == 1
    b_p1 = pltpu.roll(b, 1, axis=0)
    b_p2 = pltpu.roll(b, 2, axis=0)
    b_n1 = pltpu.roll(b, C - 1, axis=0)
    b_n2 = pltpu.roll(b, C - 2, axis=0)
    rows_of = lambda a, i, n: a[i * n:(i + 1) * n]
    sc = [None] * (C // SUB)

    def add_rows(first_row, p):
        for i in range(p.shape[0] // SUB):
            j = first_row // SUB + i
            blk = p[i * SUB:(i + 1) * SUB]
            sc[j] = blk if sc[j] is None else sc[j] + blk

    for lv in range(CHUNK_LEVELS):
        half = 1 << lv
        if half < SUB:
            sgn = sgn_ref[d, lv]
            if lv == 0:
                bm = (b_n1 if rev else b_p1) * coef_ref[d, 0] + b * coef_ref[d, 1]
            elif lv == 1:
                far = b_n2 if rev else b_p2
                bm = far * coef_ref[d, 2] + b_p1 * coef_ref[d, 3] + b * coef_ref[d, 4] + b_n1 * coef_ref[d, 5]
            else:
                nblk = C // (2 * half)
                b3 = b.reshape(nblk, 2 * half, HG_DIM)
                ref_row = half if rev else half - 1
                bm = jnp.broadcast_to(b3[:, ref_row:ref_row + 1, :], (nblk, 2 * half, HG_DIM)).reshape(C, HG_DIM)
            x = (jnp.where(sgn > 0, q, k) * jnp.exp2((b - bm) * sgn)).astype(BF16)
            add_rows(0, _dot_nt(x, x) * mask_ref[d, lv])
        else:
            xs, xq, q_first = [], [], []
            for blk in range(C // half):
                is_q = (blk % 2 == 0) if rev else (blk % 2 == 1)
                pair0 = (blk // 2) * 2 * half
                ref = pair0 + (half if rev else half - 1)
                bb = rows_of(b, blk, half)
                bm = b[ref:ref + 1, :]
                xb = ((rows_of(q, blk, half) * jnp.exp2(bb - bm)) if is_q
                      else (rows_of(k, blk, half) * jnp.exp2(bm - bb))).astype(BF16)
                xs.append(xb)
                if is_q:
                    xq.append(xb)
                    q_first.append(blk * half)
            p = _dot_nt(jnp.concatenate(xq, axis=0), jnp.concatenate(xs, axis=0))
            for i, r0 in enumerate(q_first):
                add_rows(r0, p[i * half:(i + 1) * half] * mask_ref[d, lv, r0:r0 + half, :])
    diag = jnp.sum(q * k, axis=-1, keepdims=True)
    scores = jnp.concatenate(sc, axis=0) + diag * mask_ref[d, CHUNK_LEVELS]
    b_end = b[0:1, :] if rev else b[C - 1:C, :]
    st = st_ref[d]
    sc_ref[d, slot] = scores.astype(BF16)
    qh_ref[d, slot] = (q * jnp.exp2(b)).astype(BF16)
    sti_ref[d, slot] = st.astype(BF16)
    kh = (k * jnp.exp2(b_end - b)).astype(BF16)
    st_ref[d] = st * jnp.exp2(b_end) + _dot_tn(v, kh)


def _hgrn_output(v, d, slot, sc_ref, qh_ref, sti_ref):
    return _dot(sc_ref[d, slot], v) + _dot_nt(qh_ref[d, slot], sti_ref[d, slot])


def _hgrn_kernel(q_ref, bf_ref, kf_ref, bb_ref, kb_ref, v_ref, g_ref, ng_ref, sgn_ref, mask_ref,
                 coef_ref, o_ref, of_ref, ob_ref, st_ref, sc_ref, qh_ref, sti_ref, *, n_ctx_chunks, n_lat_chunks):
    nc, nl = n_ctx_chunks, n_lat_chunks
    n = nc + nl
    st_ref[...] = jnp.zeros(st_ref.shape, F32)
    sc_ref[...] = jnp.zeros(sc_ref.shape, BF16)
    qh_ref[...] = jnp.zeros(qh_ref.shape, BF16)
    sti_ref[...] = jnp.zeros(sti_ref.shape, BF16)
    b_refs = (bf_ref, bb_ref)
    k_refs = (kf_ref, kb_ref)
    o_refs = (of_ref, ob_ref)
    consts = (st_ref, sgn_ref, mask_ref, coef_ref, sc_ref, qh_ref, sti_ref)

    def rows_of_step(j, d):
        c = j if d == 0 else jnp.where(j < nc, nc - 1 - j, 2 * nc + nl - 1 - j)
        return pl.ds(pl.multiple_of(c * CHUNK, CHUNK), CHUNK)

    def first_half(j, slot):
        for d in range(2):
            r = rows_of_step(j, d)
            _hgrn_scores(q_ref[0, r, :], b_refs[d][0, r, :], k_refs[d][0, r, :], v_ref[0, r, :], d, slot, *consts)

    def second_half(j, slot):
        for d in range(2):
            r = rows_of_step(j, d)
            o_refs[d][r, :] = _hgrn_output(v_ref[0, r, :], d, slot, sc_ref, qh_ref, sti_ref)

    def trip(i, carry):
        j0 = 2 * i
        first_half(j0, 0)
        second_half(jnp.maximum(j0 - 1, 0), 1)
        first_half(j0 + 1, 1)
        second_half(j0, 0)
        return carry

    lax.fori_loop(0, n // 2, trip, 0)
    second_half(n - 1, 1)

    def readout(j, carry):
        rows = pl.ds(pl.multiple_of(j * CHUNK, CHUNK), CHUNK)
        tot = of_ref[rows, :] + ob_ref[rows, :]
        y = tot * lax.rsqrt(jnp.mean(tot * tot, axis=-1, keepdims=True) + RMS_EPS) * ng_ref[...]
        o_ref[0, rows, :] = (y * _silu(g_ref[0, rows, :])).astype(BF16)
        return carry

    lax.fori_loop(0, n, readout, 0)


def _hgrn(hq, bf, kf, bb, kb, hi, hgate, norm_g, consts, n_ctx):
    B, S, _ = hq.shape
    assert (S // CHUNK) % 2 == 0
    col = pl.BlockSpec((1, S, HG_DIM), lambda b, h: (b, 0, h))
    return pl.pallas_call(
        functools.partial(_hgrn_kernel, n_ctx_chunks=n_ctx // CHUNK, n_lat_chunks=(S - n_ctx) // CHUNK),
        grid=(B, HG_HEADS),
        in_specs=[col, col, col, col, col, col, col,
                  pl.BlockSpec((1, HG_DIM), lambda b, h: (0, 0))] + [_resident(a.shape) for a in consts],
        out_specs=col,
        out_shape=jax.ShapeDtypeStruct((B, S, HG_W), BF16),
        scratch_shapes=[pltpu.VMEM((S, HG_DIM), F32), pltpu.VMEM((S, HG_DIM), F32),
                        pltpu.VMEM((2, HG_DIM, HG_DIM), F32), pltpu.VMEM((2, 2, CHUNK, CHUNK), BF16),
                        pltpu.VMEM((2, 2, CHUNK, HG_DIM), BF16), pltpu.VMEM((2, 2, HG_DIM, HG_DIM), BF16)],
        compiler_params=_cparams(2),
        name="hgrn2",
    )(hq, bf, kf, bb, kb, hi, hgate, norm_g, *consts)


RING = 4
MIXER_FFN_LAG = 2
CONF_FFN_LAG = 4
CONF_ROWS = 128


def _segment_edges(i, n_tiles, n_ctx_tiles):
    is_first = (i == 0) | (i == n_ctx_tiles)
    is_last = (i == n_tiles - 1) | (i == n_ctx_tiles - 1)
    return is_first, is_last


def _ffn_tile(x, xp, xn, mod_ref, is_first, is_last, wup_ref, dww_ref, dwb_ref, wdn_ref, lng_ref, lnb_ref):
    shift = mod_ref[0, 0, 3:4, :]
    scale = mod_ref[0, 0, 4:5, :]
    gate_mod = mod_ref[0, 0, 5:6, :]
    mod = lambda a: a * (1.0 + scale) + shift
    hp = jnp.where(is_first, 0.0, mod(xp))
    hn = jnp.where(is_last, 0.0, mod(xn))
    h_ext = jnp.concatenate([hp, mod(x), hn], axis=0).astype(BF16)
    n_ext = TM + 2 * FFN_HALO
    ug = _dot(h_ext, wup_ref[:, 0:FFN_HIDDEN])
    uv = _dot(mod(x).astype(BF16), wup_ref[:, FFN_HIDDEN:])
    lo = pltpu.roll(ug, 1, axis=0)[FFN_HALO:FFN_HALO + TM]
    hi = pltpu.roll(ug, n_ext - 1, axis=0)[FFN_HALO:FFN_HALO + TM]
    conv = (lo * dww_ref[0:1, :] + ug[FFN_HALO:FFN_HALO + TM] * dww_ref[1:2, :] + hi * dww_ref[2:3, :]
            + dwb_ref[...])
    act = (_silu(conv) * uv).astype(BF16)
    y = _dot(act, wdn_ref[...])
    out = _layer_norm(DEEPNORM_ALPHA * x + gate_mod * y, lng_ref[...], lnb_ref[...])
    marks = ([ug[0:SUBLANES, c:c + LANES] for c in range(0, FFN_HIDDEN, MXU_N)]
             + [uv[0:SUBLANES, c:c + LANES] for c in range(0, FFN_HIDDEN, 2 * MXU_N)][:5])
    return out, marks


def _ring_tiles(s, ring_ref, lag):
    cur = (s + 2 * RING - lag) % RING
    prv = (s + 2 * RING - lag - 1) % RING
    nxt = (s + 2 * RING - lag + 1) % RING
    return ring_ref[cur], ring_ref[prv, TM - FFN_HALO:TM, :], ring_ref[nxt, 0:FFN_HALO, :]


def _lagged_edges(s, lag, n_tiles, n_ctx_tiles, n_total):
    t = jnp.clip(s - lag, 0, n_total - 1)
    return _segment_edges(t % n_tiles, n_tiles, n_ctx_tiles)


def _conf_glu(x, xp, xn, mod_ref, is_first, is_last, w1_ref, b1_ref):
    shift = mod_ref[0, 0, 0:1, :]
    scale = mod_ref[0, 0, 1:2, :]
    n_ext = TM + 2 * CONF_HALO
    x_ext = jnp.concatenate([xp, x, xn], axis=0)
    h_ext = (x_ext * (1.0 + scale) + shift).astype(BF16)
    a = _dot(h_ext, w1_ref[...]) + b1_ref[...]
    a = a[:, :D_MODEL] * _sigmoid(a[:, D_MODEL:])
    rows = lax.broadcasted_iota(jnp.int32, (n_ext, 1), 0)
    pad = ((rows < CONF_HALO) & is_first) | ((rows >= CONF_HALO + TM) & is_last)
    return jnp.where(pad, 0.0, a)


def _conf_dwconv(a_ref, ia, c_ref, ic, dww_ref, dwb_ref, marks, zero_ref):
    half = (CONF_KERNEL - 1) // 2
    base = CONF_HALO - half
    n_slab = CONF_ROWS + 2 * CONF_HALO

    for cb in range(D_MODEL // LANES):
        lanes = slice(cb * LANES, (cb + 1) * LANES)
        for rb in range(TM // CONF_ROWS):
            slab = a_ref[ia, rb * CONF_ROWS:rb * CONF_ROWS + n_slab, lanes]
            mark = marks[(cb * (TM // CONF_ROWS) + rb) % len(marks)]
            edge = pltpu.bitcast(pltpu.bitcast(mark, jnp.int32) & zero_ref[...], F32)
            acc = jnp.broadcast_to(dwb_ref[:, lanes] + edge[0:1, :], (CONF_ROWS, LANES))
            for r in range(SUBLANES):
                sh = pltpu.roll(slab, n_slab - r, axis=0) if r else slab
                for mm in range(-(-(base + CONF_KERNEL) // SUBLANES)):
                    kk = SUBLANES * mm + r - base
                    if 0 <= kk < CONF_KERNEL:
                        acc = acc + sh[SUBLANES * mm:SUBLANES * mm + CONF_ROWS] * dww_ref[kk:kk + 1, lanes]
            c_ref[ic, rb * CONF_ROWS:(rb + 1) * CONF_ROWS, lanes] = acc


def _conf_out(conv, x, mod_ref, cg_ref, cb_ref, w2_ref, b2_ref, lng_ref, lnb_ref):
    gate_mod = mod_ref[0, 0, 2:3, :]
    z = _silu(_layer_norm(conv, cg_ref[...], cb_ref[...])).astype(BF16)
    y = _dot(z, w2_ref[...]) + b2_ref[...]
    return _layer_norm(DEEPNORM_ALPHA * x + gate_mod * y, lng_ref[...], lnb_ref[...])


def _zero_once(s, *refs):
    @pl.when(s == 0)
    def _():
        for ref in refs:
            ref[...] = jnp.zeros(ref.shape, F32)


def _conf_ffn_kernel(x_ref, xp_ref, xn_ref, xres_ref, moda_ref, modc_ref, zero_ref, modf_ref,
                     w1_ref, b1_ref, cdww_ref, cdwb_ref, cg_ref, cb_ref, w2_ref, b2_ref, tlng_ref, tlnb_ref,
                     wup_ref, fdww_ref, fdwb_ref, wdn_ref, flng_ref, flnb_ref,
                     o_ref, ring_ref, a_ref, c_ref, *, n_tiles, n_ctx_tiles, n_total):
    s = pl.program_id(0)
    _zero_once(s, ring_ref, a_ref, c_ref)
    edges = functools.partial(_lagged_edges, s, n_tiles=n_tiles, n_ctx_tiles=n_ctx_tiles, n_total=n_total)
    ffn_in = _ring_tiles(s, ring_ref, CONF_FFN_LAG)
    ring_ref[(s + 2) % RING] = _conf_out(c_ref[s % 2], xres_ref[0], modc_ref, cg_ref, cb_ref, w2_ref, b2_ref,
                                         tlng_ref, tlnb_ref)
    o_ref[0], marks = _ffn_tile(*ffn_in, modf_ref, *edges(CONF_FFN_LAG),
                                wup_ref, fdww_ref, fdwb_ref, wdn_ref, flng_ref, flnb_ref)
    _conf_dwconv(a_ref, (s + 1) % 2, c_ref, (s + 1) % 2, cdww_ref, cdwb_ref, marks, zero_ref)
    a_ref[s % 2] = _conf_glu(x_ref[0], xp_ref[0], xn_ref[0], moda_ref, *edges(0), w1_ref, b1_ref)


def _mixout_ffn_kernel(att_ref, hg_ref, *refs, n_streams, n_tiles, n_ctx_tiles, n_total):
    x_refs, refs = refs[:n_streams], refs[n_streams:]
    (modt_ref, modf_ref, wo_ref, tlng_ref, tlnb_ref, wup_ref, fdww_ref, fdwb_ref, wdn_ref, flng_ref, flnb_ref,
     o_ref, ring_ref) = refs
    s = pl.program_id(0)
    _zero_once(s, ring_ref)
    ffn_in = _ring_tiles(s, ring_ref, MIXER_FFN_LAG)
    y = _dot(att_ref[0], wo_ref[0:ATT_Q_W, :]) + _dot(hg_ref[0], wo_ref[ATT_Q_W:, :])
    o_ref[0], _ = _ffn_tile(*ffn_in, modf_ref, *_lagged_edges(s, MIXER_FFN_LAG, n_tiles, n_ctx_tiles, n_total),
                            wup_ref, fdww_ref, fdwb_ref, wdn_ref, flng_ref, flnb_ref)
    gate = modt_ref[0, 0, 2:3, :]
    x = _stream_tile(x_refs, jnp.minimum(s, n_total - 1) % n_tiles, n_ctx_tiles)
    ring_ref[s % RING] = _layer_norm(DEEPNORM_ALPHA * x + gate * y, tlng_ref[...], tlnb_ref[...])


def _tail_ffn(kind, acts, mods, tail_params, ffn_params, n_in_ctx_tiles, skip_ctx):
    B, S, _ = acts[0].shape
    off = n_in_ctx_tiles if skip_ctx else 0
    nt = S // TM - off
    nct = 0 if skip_ctx else n_in_ctx_tiles
    n_total = B * nt
    ffn_lag = CONF_FFN_LAG if kind == "conformer" else MIXER_FFN_LAG

    def lagged(lag):
        def tile(s):
            t = jnp.clip(s - lag, 0, n_total - 1)
            return t // nt, t % nt
        return tile

    tail_tile = lagged(0)

    def rows(w, tile_fn=tail_tile):
        def index(s):
            b, i = tile_fn(s)
            return (b, i + off, 0)
        return pl.BlockSpec((1, TM, w), index)

    def halo(before):
        per = TM // CONF_HALO

        def index(s):
            b, i = tail_tile(s)
            blk = (i + off) * per - 1 if before else (i + off + 1) * per
            return (b, jnp.clip(blk, 0, (S // TM) * per - 1), 0)
        return pl.BlockSpec((1, CONF_HALO, D_MODEL), index)

    def mod_spec(tile_fn):
        def index(s):
            b, i = tile_fn(s)
            return (b, jnp.where(i < nct, 0, 1), 0, 0)
        return pl.BlockSpec((1, 1, 6, D_MODEL), index)

    def out_index(s):
        b, i = lagged(ffn_lag)(s)
        return (b, i, 0)

    small = lambda a: pl.BlockSpec(a.shape, lambda s: (0,) * a.ndim)
    spec_of = lambda a: _resident(a.shape) if a.size * a.dtype.itemsize > (1 << 20) else small(a)
    scratch = [pltpu.VMEM((RING, TM, D_MODEL), F32)]
    if kind == "conformer":
        xs, = acts
        body = _conf_ffn_kernel
        zero = jnp.zeros((SUBLANES, LANES), jnp.int32)
        in_specs = [rows(D_MODEL), halo(True), halo(False), rows(D_MODEL, lagged(2)),
                    mod_spec(tail_tile), mod_spec(lagged(2)), small(zero)]
        operands = [xs, xs, xs, xs, mods, mods, zero]
        scratch += [pltpu.VMEM((2, TM + 2 * CONF_HALO, D_MODEL), F32), pltpu.VMEM((2, TM, D_MODEL), F32)]
    else:
        streams = acts[2:]
        assert len(streams) == 1 or not skip_ctx
        body = functools.partial(_mixout_ffn_kernel, n_streams=len(streams))
        in_specs = [rows(a.shape[2]) for a in acts[:2]]
        in_specs += [rows(D_MODEL)] if len(streams) == 1 else _stream_specs(streams, nct, tail_tile)
        in_specs += [mod_spec(tail_tile)]
        operands = list(acts) + [mods]
    in_specs += [mod_spec(lagged(ffn_lag))] + [spec_of(a) for a in tail_params + ffn_params]
    operands += [mods] + list(tail_params) + list(ffn_params)
    return pl.pallas_call(
        functools.partial(body, n_tiles=nt, n_ctx_tiles=nct, n_total=n_total),
        grid=(n_total + ffn_lag,),
        in_specs=in_specs,
        out_specs=pl.BlockSpec((1, TM, D_MODEL), out_index),
        out_shape=jax.ShapeDtypeStruct((B, nt * TM, D_MODEL), F32),
        scratch_shapes=scratch,
        compiler_params=_cparams(1),
        name=kind + "_ffn",
    )(*operands)


def _rope_tables(n_ctx, n_lat):
    rows = n_lat // GRID_W
    row = jnp.repeat(jnp.arange(rows), GRID_W).astype(F32)
    col = jnp.tile(jnp.arange(GRID_W), rows).astype(F32)
    quarter = ATT_HEAD_DIM // 4
    inv_freq = ROPE_BASE ** (-jnp.arange(quarter, dtype=F32) / quarter)
    ang_r = row[:, None] * inv_freq
    ang_c = col[:, None] * inv_freq
    ang = jnp.concatenate([ang_r, ang_r, ang_c, ang_c], axis=-1)
    ang = jnp.concatenate([jnp.zeros((n_ctx, ATT_HEAD_DIM), F32), ang], axis=0)
    ang = jnp.concatenate([ang, ang], axis=-1)
    cos, sin = jnp.cos(ang), jnp.sin(ang)
    low = (jnp.arange(LANES) % (2 * quarter)) < quarter
    return cos, jnp.where(low, -sin, 0.0), jnp.where(low, 0.0, sin)


def _chunk_constants():
    t = jnp.arange(CHUNK)
    full = lambda col: jnp.broadcast_to(col[:, None], (CHUNK, HG_DIM)).astype(F32)
    sgn, mask = [], []
    for d in range(2):
        sgn_d, mask_d = [], []
        for lv in range(CHUNK_LEVELS):
            is_q = ((t >> lv) & 1) == (1 - d)
            same = (t[:, None] >> (lv + 1)) == (t[None, :] >> (lv + 1))
            sgn_d.append(full(jnp.where(is_q, 1.0, -1.0)))
            mask_d.append((same & is_q[:, None] & ~is_q[None, :]).astype(F32))
        mask_d.append((t[:, None] == t[None, :]).astype(F32))
        sgn.append(jnp.stack(sgn_d))
        mask.append(jnp.stack(mask_d))
    odd = (t & 1) == 1
    r = t & 3
    coef = jnp.stack([
        jnp.stack([full(odd), full(~odd), full(r == 3), full(r == 2), full(r == 1), full(r == 0)]),
        jnp.stack([full(~odd), full(odd), full(r == 0), full(r == 3), full(r == 2), full(r == 1)]),
    ])
    return jnp.stack(sgn), jnp.stack(mask), coef


def _prefix_matrices():
    t = jnp.arange(CHUNK)
    lower = t[None, :] <= t[:, None]
    return jnp.stack([lower, lower.T]).astype(BF16)


def kernel(x, c, ctx, c_ctx, mod_w, mod_b, post_ln_g, post_ln_b, mix_w_in, mix_w_out, att_sink, hg_lb_logits, hg_norm_g, conf_pw1_w, conf_pw1_b, conf_dw_w, conf_dw_b, conf_ln_g, conf_ln_b, conf_pw2_w, conf_pw2_b, ffn_w_up, ffn_dw_w, ffn_dw_b, ffn_w_down):
    B, T, D = x.shape
    L = ctx.shape[1]
    assert D == D_MODEL and L % TM == 0 and T % TM == 0 and L >= ATT_BLOCK and T % GRID_W == 0
    n_ctx_tiles = L // TM

    n_cond = -(-(B + 1) // SUBLANES) * SUBLANES
    cond = jnp.zeros((n_cond, D), F32).at[:B].set(c).at[B].set(c_ctx)
    mod = _modulation(cond, mod_w.astype(BF16), mod_b[:, None, :])
    mod_x = mod[:, :B].reshape(DEPTH, B, 1, 6, D)
    mod_c = jnp.broadcast_to(mod[:, B].reshape(DEPTH, 1, 1, 6, D), (DEPTH, B, 1, 6, D))
    mods = jnp.concatenate([mod_c, mod_x], axis=2)

    cos, sa, sb = _rope_tables(L, T)
    chunk_consts = _chunk_constants()
    tri = _prefix_matrices()
    vec = lambda a: a.reshape(1, -1)

    streams = (ctx, x)
    has_ctx = True
    for layer in range(DEPTH):
        m = layer // 2
        need_ctx_out = any(j % 2 == 0 for j in range(layer + 1, DEPTH))
        nct = n_ctx_tiles if has_ctx else 0
        lg, lb = post_ln_g[layer], post_ln_b[layer]
        ffn_params = (ffn_w_up[layer].astype(BF16), ffn_dw_w[layer], vec(ffn_dw_b[layer]),
                      ffn_w_down[layer].astype(BF16), vec(lg[1]), vec(lb[1]))
        if layer % 2 == 0:
            q, k, v, hq, bf, kf, bb, kb, hi, hgate = _mix_in(
                streams, mods[layer], mix_w_in[m].astype(BF16), cos, sa, sb, hg_lb_logits, tri, m, nct)
            att = _attention(att_sink[m], q, k, v, L, need_ctx_out)
            hg = _hgrn(hq, bf, kf, bb, kb, hi, hgate, vec(hg_norm_g[m]), chunk_consts, L)
            tail_params = (mix_w_out[m].astype(BF16), vec(lg[0]), vec(lb[0]))
            xs = _tail_ffn("mixer", (att, hg) + streams, mods[layer], tail_params, ffn_params, nct,
                           not need_ctx_out)
        else:
            tail_params = (conf_pw1_w[m].astype(BF16), vec(conf_pw1_b[m]), conf_dw_w[m], vec(conf_dw_b[m]),
                           vec(conf_ln_g[m]), vec(conf_ln_b[m]), conf_pw2_w[m].astype(BF16), vec(conf_pw2_b[m]),
                           vec(lg[0]), vec(lb[0]))
            xs = _tail_ffn("conformer", streams, mods[layer], tail_params, ffn_params, nct,
                           has_ctx and not need_ctx_out)
        streams = (xs,)
        has_ctx = has_ctx and need_ctx_out
    return xs[:, L:] if has_ctx else xs
```

```python
import functools

import jax
import jax.numpy as jnp
from jax import lax
from jax.experimental import pallas as pl
from jax.experimental.pallas import tpu as pltpu

F32 = jnp.float32
BF16 = jnp.bfloat16

D_MODEL = 1024
DEPTH = 4
GRID_W = 64
ATT_HEADS = 8
ATT_KV_HEADS = 2
ATT_HEAD_DIM = 64
ATT_WINDOW = 128
ATT_BLOCK = 128
ROPE_BASE = 10000.0
HG_HEADS = 4
HG_DIM = 128
CONF_KERNEL = 31
FFN_HIDDEN = 2816
DEEPNORM_ALPHA = (2 * DEPTH) ** 0.25
LN_EPS = 1e-5
RMS_EPS = 1e-6
MASK_VALUE = -1e30
LB_FLOOR = 1e-30

ATT_Q_W = ATT_HEADS * ATT_HEAD_DIM
ATT_KV_W = ATT_KV_HEADS * ATT_HEAD_DIM
HG_W = HG_HEADS * HG_DIM
OFF_AQ = 0
OFF_AK = OFF_AQ + ATT_Q_W
OFF_AV = OFF_AK + ATT_KV_W
OFF_HQ = OFF_AV + ATT_KV_W
OFF_FF = OFF_HQ + HG_W
OFF_FB = OFF_FF + HG_W
OFF_HI = OFF_FB + HG_W
OFF_HGATE = OFF_HI + HG_W
MIX_IN_W = OFF_HGATE + HG_W

LANES = 128
SUBLANES = 8
MXU_N = 256
TM = 256
CHUNK = 128
CHUNK_LEVELS = 7
READOUT_UNROLL = 6
FFN_HALO = SUBLANES
CONF_HALO = 16
VMEM_LIMIT = 56 * 1024 * 1024


def _cparams(n_grid):
    return pltpu.CompilerParams(
        dimension_semantics=("arbitrary",) * n_grid, vmem_limit_bytes=VMEM_LIMIT)


def _dot(a, b):
    return jnp.dot(a, b, preferred_element_type=F32)


def _dot_nt(a, b):
    return lax.dot_general(a, b, (((1,), (1,)), ((), ())), preferred_element_type=F32)


def _dot_tn(a, b):
    return lax.dot_general(a, b, (((0,), (0,)), ((), ())), preferred_element_type=F32)


def _sigmoid(x):
    return 0.5 * jnp.tanh(0.5 * x) + 0.5


def _silu(x):
    h = 0.5 * x
    return h * (1.0 + jnp.tanh(h))


def _layer_norm(z, g, b):
    mu = jnp.mean(z, axis=-1, keepdims=True)
    zc = z - mu
    var = jnp.mean(zc * zc, axis=-1, keepdims=True)
    return zc * lax.rsqrt(var + LN_EPS) * g + b


def _resident(shape):
    nd = len(shape)
    return pl.BlockSpec(shape, lambda *_: (0,) * nd, pipeline_mode=pl.Buffered(1))


MOD_TN = 1536


def _mod_kernel(cond_ref, w_ref, b_ref, o_ref):
    s = _silu(cond_ref[...])
    hi = s.astype(BF16)
    lo = (s - hi.astype(F32)).astype(BF16)
    w = w_ref[0]
    o_ref[0] = _dot(hi, w) + _dot(lo, w) + b_ref[0]


def _modulation(cond, mod_w, mod_b):
    R = cond.shape[0]
    n6 = mod_w.shape[2]
    return pl.pallas_call(
        _mod_kernel,
        grid=(DEPTH, n6 // MOD_TN),
        in_specs=[
            pl.BlockSpec((R, D_MODEL), lambda l, j: (0, 0)),
            pl.BlockSpec((1, D_MODEL, MOD_TN), lambda l, j: (l, 0, j)),
            pl.BlockSpec((1, 1, MOD_TN), lambda l, j: (l, 0, j)),
        ],
        out_specs=pl.BlockSpec((1, R, MOD_TN), lambda l, j: (l, 0, j)),
        out_shape=jax.ShapeDtypeStruct((DEPTH, R, n6), F32),
        compiler_params=_cparams(2),
        name="modulation",
    )(cond, mod_w, mod_b)


def _stream_tile(x_refs, i, n_ctx_tiles):
    if len(x_refs) == 1:
        return x_refs[0][0]
    return jnp.where(i < n_ctx_tiles, x_refs[0][0], x_refs[1][0])


def _stream_specs(streams, n_ctx_tiles, tile_fn):
    if len(streams) == 1:
        def index(*g):
            b, i = tile_fn(*g)
            return (b, i, 0)
        return [pl.BlockSpec((1, TM, D_MODEL), index)]

    def ctx_index(*g):
        b, i = tile_fn(*g)
        return (b, jnp.minimum(i, n_ctx_tiles - 1), 0)

    def lat_index(*g):
        b, i = tile_fn(*g)
        return (b, jnp.maximum(i - n_ctx_tiles, 0), 0)
    return [pl.BlockSpec((1, TM, D_MODEL), ctx_index), pl.BlockSpec((1, TM, D_MODEL), lat_index)]


def _mix_in_kernel(*refs, layer_m, n_streams, n_ctx_tiles):
    x_refs, refs = refs[:n_streams], refs[n_streams:]
    (mod_ref, w_ref, cos_ref, sa_ref, sb_ref, lbl_ref, tri_ref,
     q_ref, k_ref, v_ref, hq_ref, bf_ref, kf_ref, bb_ref, kb_ref, hi_ref, hg_ref) = refs
    x = _stream_tile(x_refs, pl.program_id(1), n_ctx_tiles)
    shift = mod_ref[0, 0, 0:1, :]
    scale = mod_ref[0, 0, 1:2, :]
    h = (x * (1.0 + scale) + shift).astype(BF16)

    def proj(off, width):
        return _dot(h, w_ref[:, off:off + width])

    fr_fwd = proj(OFF_FF, HG_W)
    fr_bwd = proj(OFF_FB, HG_W)

    cos = cos_ref[...]
    sa = sa_ref[...]
    sb = sb_ref[...]

    def rope(a, reps):
        w = a.shape[1]
        c = jnp.concatenate([cos] * reps, axis=1) if reps > 1 else cos
        s1 = jnp.concatenate([sa] * reps, axis=1) if reps > 1 else sa
        s2 = jnp.concatenate([sb] * reps, axis=1) if reps > 1 else sb
        up = pltpu.roll(a, w - 16, axis=1)
        dn = pltpu.roll(a, 16, axis=1)
        return a * c + up * s1 + dn * s2

    q = rope(proj(OFF_AQ, ATT_Q_W), ATT_Q_W // LANES) * (ATT_HEAD_DIM ** -0.5)
    q_ref[0] = q.astype(BF16)

    lane = lax.broadcasted_iota(jnp.int32, (TM, LANES), 1)
    first = lane < ATT_HEAD_DIM

    def pair_rep(a):
        sw = pltpu.roll(a, ATT_HEAD_DIM, axis=1)
        return jnp.concatenate([jnp.where(first, a, sw), jnp.where(first, sw, a)], axis=1)

    k_ref[0] = pair_rep(rope(proj(OFF_AK, ATT_KV_W), 1)).astype(BF16)
    v_ref[0] = pair_rep(proj(OFF_AV, ATT_KV_W)).astype(BF16)

    logits = lbl_ref[...]
    n_mix = logits.shape[0]
    mx = logits[0:1, :]
    for r in range(1, n_mix):
        mx = jnp.maximum(mx, logits[r:r + 1, :])
    ex = [jnp.exp(logits[r:r + 1, :] - mx) for r in range(n_mix)]
    tot = ex[0]
    for r in range(1, n_mix):
        tot = tot + ex[r]
    cum = ex[0] / tot
    p0 = cum
    for r in range(1, layer_m + 1):
        cum = cum + ex[r] / tot
    lb = cum - p0
    lb_floor = jnp.maximum(lb, LB_FLOOR)
    one_m = 1.0 - lb

    def gates(fr, d, b_out, k_out):
        e = jnp.exp(-jnp.abs(fr))
        r = 1.0 / (1.0 + e)
        er = e * r
        pos = fr >= 0
        k_out[0] = one_m * jnp.where(pos, er, r)
        lf = jnp.log2(lb_floor + one_m * jnp.where(pos, r, er))
        tri = tri_ref[d]
        for c in range(TM // CHUNK):
            g = lf[c * CHUNK:(c + 1) * CHUNK]
            g_hi = g.astype(BF16)
            g_lo = (g - g_hi.astype(F32)).astype(BF16)
            b_out[0, c * CHUNK:(c + 1) * CHUNK, :] = _dot(tri, g_hi) + _dot(tri, g_lo)

    hq_ref[0] = proj(OFF_HQ, HG_W)
    gates(fr_fwd, 0, bf_ref, kf_ref)
    hi_ref[0] = proj(OFF_HI, HG_W).astype(BF16)
    gates(fr_bwd, 1, bb_ref, kb_ref)
    hg_ref[0] = proj(OFF_HGATE, HG_W)


def _mix_in(streams, mods, w_in, cos, sa, sb, lb_logits, tri, layer_m, n_ctx_tiles):
    B = streams[0].shape[0]
    S = sum(a.shape[1] for a in streams)
    nt = S // TM
    row = lambda w: pl.BlockSpec((1, TM, w), lambda b, i: (b, i, 0))
    tab = pl.BlockSpec((TM, LANES), lambda b, i: (i, 0))
    sds = lambda w, dt: jax.ShapeDtypeStruct((B, S, w), dt)
    return pl.pallas_call(
        functools.partial(_mix_in_kernel, layer_m=layer_m, n_streams=len(streams), n_ctx_tiles=n_ctx_tiles),
        grid=(B, nt),
        in_specs=_stream_specs(streams, n_ctx_tiles, lambda b, i: (b, i)) + [
            pl.BlockSpec((1, 1, 6, D_MODEL), lambda b, i: (b, jnp.where(i < n_ctx_tiles, 0, 1), 0, 0)),
            _resident((D_MODEL, MIX_IN_W)),
            tab, tab, tab,
            _resident(lb_logits.shape),
            _resident(tri.shape),
        ],
        out_specs=[row(ATT_Q_W), row(2 * LANES), row(2 * LANES), row(HG_W), row(HG_W), row(HG_W),
                   row(HG_W), row(HG_W), row(HG_W), row(HG_W)],
        out_shape=[sds(ATT_Q_W, BF16), sds(2 * LANES, BF16), sds(2 * LANES, BF16), sds(HG_W, F32),
                   sds(HG_W, F32), sds(HG_W, F32), sds(HG_W, F32), sds(HG_W, F32), sds(HG_W, BF16),
                   sds(HG_W, F32)],
        compiler_params=_cparams(2),
        name="mix_in",
    )(*streams, mods, w_in, cos, sa, sb, lb_logits, tri)


GROUP = ATT_HEADS // ATT_KV_HEADS


def _attn_kernel(sink_ref, q_ref, k_ref, v_ref, o_ref, p_ref, rden_ref, *, n_ctx, n_lat, ctx_out):
    hkv = pl.program_id(1)
    QB = ATT_BLOCK
    nb = n_lat // QB
    lane = lax.broadcasted_iota(jnp.int32, (QB, LANES), 1)
    first = lane < ATT_HEAD_DIM
    rows4 = lax.broadcasted_iota(jnp.int32, (GROUP * QB, LANES), 0)
    rq = rows4 & (QB - 1)
    col = lax.broadcasted_iota(jnp.int32, (GROUP * QB, LANES), 1)
    grp = lax.broadcasted_iota(jnp.int32, (GROUP * QB, 1), 0) // QB
    sink = jnp.zeros((GROUP * QB, 1), F32)
    for g in range(GROUP):
        sink = jnp.where(grp == g, sink_ref[hkv * GROUP + g], sink)

    def stack_q(q):
        qa, qb = q[:, :LANES], q[:, LANES:]
        z = jnp.zeros_like(qa)
        return jnp.concatenate([jnp.where(first, qa, z), jnp.where(first, z, qa),
                                jnp.where(first, qb, z), jnp.where(first, z, qb)], axis=0)

    def unstack_o(o):
        return jnp.concatenate([jnp.where(first, o[0:QB], o[QB:2 * QB]),
                                jnp.where(first, o[2 * QB:3 * QB], o[3 * QB:4 * QB])], axis=1)

    kc = k_ref[0, 0:n_ctx, :]
    vc = v_ref[0, 0:n_ctx, :]
    tiles = lambda a: [a[:, c:c + LANES] for c in range(0, a.shape[1], LANES)]

    def softmax_parts(parts):
        mt = None
        for s in parts:
            for t in tiles(s):
                mt = t if mt is None else jnp.maximum(mt, t)
        m = jnp.maximum(sink, jnp.max(mt, axis=-1, keepdims=True))
        dt = None
        ps = []
        for s in parts:
            p = jnp.exp(s - m)
            for t in tiles(p):
                dt = t if dt is None else dt + t
            ps.append(p.astype(BF16))
        den = jnp.exp(sink - m) + jnp.sum(dt, axis=-1, keepdims=True)
        return ps, 1.0 / den

    def block_rows(i):
        r0 = pl.multiple_of(n_ctx + i * QB, QB)
        rp = pl.multiple_of(r0 - QB, QB)
        rn = pl.multiple_of(jnp.minimum(r0 + QB, n_ctx + n_lat - QB), QB)
        return rp, r0, rn

    def qk_scores(i):
        rp, r0, rn = block_rows(i)
        q4 = stack_q(q_ref[0, pl.ds(r0, QB), :])
        lo_col = jnp.where(i > 0, rq, LANES)
        hi_col = jnp.where(i < nb - 1, rq, -1)
        sp = jnp.where(col >= lo_col, _dot_nt(q4, k_ref[0, pl.ds(rp, QB), :]), MASK_VALUE)
        ss = _dot_nt(q4, k_ref[0, pl.ds(r0, QB), :])
        sn = jnp.where(col <= hi_col, _dot_nt(q4, k_ref[0, pl.ds(rn, QB), :]), MASK_VALUE)
        return [sp, ss, sn, _dot_nt(q4, kc)]

    def store_probs(parts, slot):
        ps, rden = softmax_parts(parts)
        p_ref[slot] = jnp.concatenate(ps, axis=1)
        rden_ref[slot] = jnp.broadcast_to(rden, (GROUP * QB, LANES))

    def output_stage(i, slot):
        rp, r0, rn = block_rows(i)
        acc = (_dot(p_ref[slot, :, 0:QB], v_ref[0, pl.ds(rp, QB), :])
               + _dot(p_ref[slot, :, QB:2 * QB], v_ref[0, pl.ds(r0, QB), :])
               + _dot(p_ref[slot, :, 2 * QB:3 * QB], v_ref[0, pl.ds(rn, QB), :])
               + _dot(p_ref[slot, :, 3 * QB:], vc))
        o_ref[0, pl.ds(r0, QB), :] = unstack_o(acc * rden_ref[slot]).astype(BF16)

    p_ref[...] = jnp.zeros(p_ref.shape, BF16)
    rden_ref[...] = jnp.zeros(rden_ref.shape, F32)

    def trip(t, carry):
        i0 = 2 * t
        s_a = qk_scores(i0)
        output_stage(jnp.maximum(i0 - 1, 0), 1)
        s_b = qk_scores(i0 + 1)
        store_probs(s_a, 0)
        output_stage(i0, 0)
        store_probs(s_b, 1)
        return carry

    lax.fori_loop(0, nb // 2, trip, 0)
    output_stage(nb - 1, 1)

    for j in range(n_ctx // QB):
        if ctx_out:
            q4 = stack_q(q_ref[0, j * QB:(j + 1) * QB, :])
            (p,), rden = softmax_parts([_dot_nt(q4, kc)])
            o_ref[0, j * QB:(j + 1) * QB, :] = unstack_o(_dot(p, vc) * rden).astype(BF16)
        else:
            o_ref[0, j * QB:(j + 1) * QB, :] = jnp.zeros((QB, 2 * LANES), BF16)


def _attention(sink, q, k, v, n_ctx, ctx_out):
    B, S, _ = q.shape
    return pl.pallas_call(
        functools.partial(_attn_kernel, n_ctx=n_ctx, n_lat=S - n_ctx, ctx_out=ctx_out),
        grid=(B, ATT_KV_HEADS),
        in_specs=[
            pl.BlockSpec(memory_space=pltpu.SMEM),
            pl.BlockSpec((1, S, 2 * LANES), lambda b, h: (b, 0, h)),
            pl.BlockSpec((1, S, LANES), lambda b, h: (b, 0, h)),
            pl.BlockSpec((1, S, LANES), lambda b, h: (b, 0, h)),
        ],
        out_specs=pl.BlockSpec((1, S, 2 * LANES), lambda b, h: (b, 0, h)),
        out_shape=jax.ShapeDtypeStruct((B, S, ATT_Q_W), BF16),
        scratch_shapes=[pltpu.VMEM((2, GROUP * ATT_BLOCK, 3 * ATT_BLOCK + n_ctx), BF16),
                        pltpu.VMEM((2, GROUP * ATT_BLOCK, LANES), F32)],
        compiler_params=_cparams(2),
        name="attention",
    )(sink, q, k, v)


def _hgrn_scores(q, b, k, v, d, slot, st_ref, sgn_ref, mask_ref, coef_ref, sc_ref, qh_ref, sti_ref):
    C = CHUNK
    SUB = SUBLANES
    rev = d == 1
    b_p1 = pltpu.roll(b, 1, axis=0)
    b_p2 = pltpu.roll(b, 2, axis=0)
    b_n1 = pltpu.roll(b, C - 1, axis=0)
    b_n2 = pltpu.roll(b, C - 2, axis=0)
    rows_of = lambda a, i, n: a[i * n:(i + 1) * n]
    sc = [None] * (C // SUB)

    def add_rows(first_row, p):
        for i in range(p.shape[0] // SUB):
            j = first_row // SUB + i
            blk = p[i * SUB:(i + 1) * SUB]
            sc[j] = blk if sc[j] is None else sc[j] + blk

    for lv in range(CHUNK_LEVELS):
        half = 1 << lv
        if half < SUB:
            sgn = sgn_ref[d, lv]
            if lv == 0:
                bm = (b_n1 if rev else b_p1) * coef_ref[d, 0] + b * coef_ref[d, 1]
            elif lv == 1:
                far = b_n2 if rev else b_p2
                bm = far * coef_ref[d, 2] + b_p1 * coef_ref[d, 3] + b * coef_ref[d, 4] + b_n1 * coef_ref[d, 5]
            else:
                nblk = C // (2 * half)
                b3 = b.reshape(nblk, 2 * half, HG_DIM)
                ref_row = half if rev else half - 1
                bm = jnp.broadcast_to(b3[:, ref_row:ref_row + 1, :], (nblk, 2 * half, HG_DIM)).reshape(C, HG_DIM)
            x = (jnp.where(sgn > 0, q, k) * jnp.exp2((b - bm) * sgn)).astype(BF16)
            add_rows(0, _dot_nt(x, x) * mask_ref[d, lv])
        else:
            xs, xq, q_first = [], [], []
            for blk in range(C // half):
                is_q = (blk % 2 == 0) if rev else (blk % 2 == 1)
                pair0 = (blk // 2) * 2 * half
                ref = pair0 + (half if rev else half - 1)
                bb = rows_of(b, blk, half)
                bm = b[ref:ref + 1, :]
                xb = ((rows_of(q, blk, half) * jnp.exp2(bb - bm)) if is_q
                      else (rows_of(k, blk, half) * jnp.exp2(bm - bb))).astype(BF16)
                xs.append(xb)
                if is_q:
                    xq.append(xb)
                    q_first.append(blk * half)
            p = _dot_nt(jnp.concatenate(xq, axis=0), jnp.concatenate(xs, axis=0))
            for i, r0 in enumerate(q_first):
                add_rows(r0, p[i * half:(i + 1) * half] * mask_ref[d, lv, r0:r0 + half, :])
    diag = jnp.sum(q * k, axis=-1, keepdims=True)
    scores = jnp.concatenate(sc, axis=0) + diag * mask_ref[d, CHUNK_LEVELS]
    b_end = b[0:1, :] if rev else b[C - 1:C, :]
    st = st_ref[d]
    sc_ref[d, slot] = scores.astype(BF16)
    qh_ref[d, slot] = (q * jnp.exp2(b)).astype(BF16)
    sti_ref[d, slot] = st.astype(BF16)
    kh = (k * jnp.exp2(b_end - b)).astype(BF16)
    st_ref[d] = st * jnp.exp2(b_end) + _dot_tn(v, kh)


def _hgrn_output(v, d, slot, sc_ref, qh_ref, sti_ref):
    return _dot(sc_ref[d, slot], v) + _dot_nt(qh_ref[d, slot], sti_ref[d, slot])


def _hgrn_kernel(q_ref, bf_ref, kf_ref, bb_ref, kb_ref, v_ref, g_ref, ng_ref, sgn_ref, mask_ref,
                 coef_ref, o_ref, of_ref, ob_ref, st_ref, sc_ref, qh_ref, sti_ref, *, n_ctx_chunks, n_lat_chunks):
    nc, nl = n_ctx_chunks, n_lat_chunks
    n = nc + nl
    st_ref[...] = jnp.zeros(st_ref.shape, F32)
    sc_ref[...] = jnp.zeros(sc_ref.shape, BF16)
    qh_ref[...] = jnp.zeros(qh_ref.shape, BF16)
    sti_ref[...] = jnp.zeros(sti_ref.shape, BF16)
    b_refs = (bf_ref, bb_ref)
    k_refs = (kf_ref, kb_ref)
    o_refs = (of_ref, ob_ref)
    consts = (st_ref, sgn_ref, mask_ref, coef_ref, sc_ref, qh_ref, sti_ref)

    def rows_of_step(j, d):
        c = j if d == 0 else jnp.where(j < nc, nc - 1 - j, 2 * nc + nl - 1 - j)
        return pl.ds(pl.multiple_of(c * CHUNK, CHUNK), CHUNK)

    def first_half(j, slot):
        for d in range(2):
            r = rows_of_step(j, d)
            _hgrn_scores(q_ref[0, r, :], b_refs[d][0, r, :], k_refs[d][0, r, :], v_ref[0, r, :], d, slot, *consts)

    def second_half(j, slot):
        for d in range(2):
            r = rows_of_step(j, d)
            o_refs[d][r, :] = _hgrn_output(v_ref[0, r, :], d, slot, sc_ref, qh_ref, sti_ref)

    def trip(i, carry):
        j0 = 2 * i
        first_half(j0, 0)
        second_half(jnp.maximum(j0 - 1, 0), 1)
        first_half(j0 + 1, 1)
        second_half(j0, 0)
        return carry

    lax.fori_loop(0, n // 2, trip, 0)
    second_half(n - 1, 1)

    def readout(j, carry):
        rows = pl.ds(pl.multiple_of(j * CHUNK, CHUNK), CHUNK)
        tot = of_ref[rows, :] + ob_ref[rows, :]
        y = tot * lax.rsqrt(jnp.mean(tot * tot, axis=-1, keepdims=True) + RMS_EPS) * ng_ref[...]
        o_ref[0, rows, :] = (y * _silu(g_ref[0, rows, :])).astype(BF16)
        return carry

    lax.fori_loop(0, n, readout, 0, unroll=READOUT_UNROLL if n % READOUT_UNROLL == 0 else 1)


def _hgrn(hq, bf, kf, bb, kb, hi, hgate, norm_g, consts, n_ctx):
    B, S, _ = hq.shape
    assert (S // CHUNK) % 2 == 0
    col = pl.BlockSpec((1, S, HG_DIM), lambda b, h: (b, 0, h))
    return pl.pallas_call(
        functools.partial(_hgrn_kernel, n_ctx_chunks=n_ctx // CHUNK, n_lat_chunks=(S - n_ctx) // CHUNK),
        grid=(B, HG_HEADS),
        in_specs=[col, col, col, col, col, col, col,
                  pl.BlockSpec((1, HG_DIM), lambda b, h: (0, 0))] + [_resident(a.shape) for a in consts],
        out_specs=col,
        out_shape=jax.ShapeDtypeStruct((B, S, HG_W), BF16),
        scratch_shapes=[pltpu.VMEM((S, HG_DIM), F32), pltpu.VMEM((S, HG_DIM), F32),
                        pltpu.VMEM((2, HG_DIM, HG_DIM), F32), pltpu.VMEM((2, 2, CHUNK, CHUNK), BF16),
                        pltpu.VMEM((2, 2, CHUNK, HG_DIM), BF16), pltpu.VMEM((2, 2, HG_DIM, HG_DIM), BF16)],
        compiler_params=_cparams(2),
        name="hgrn2",
    )(hq, bf, kf, bb, kb, hi, hgate, norm_g, *consts)


RING = 4
MIXER_FFN_LAG = 2
CONF_FFN_LAG = 4
CONF_ROWS = 128


def _segment_edges(i, n_tiles, n_ctx_tiles):
    is_first = (i == 0) | (i == n_ctx_tiles)
    is_last = (i == n_tiles - 1) | (i == n_ctx_tiles - 1)
    return is_first, is_last


def _ffn_tile(x, xp, xn, mod_ref, is_first, is_last, wup_ref, dww_ref, dwb_ref, wdn_ref, lng_ref, lnb_ref):
    shift = mod_ref[0, 0, 3:4, :]
    scale = mod_ref[0, 0, 4:5, :]
    gate_mod = mod_ref[0, 0, 5:6, :]
    mod = lambda a: a * (1.0 + scale) + shift
    hp = jnp.where(is_first, 0.0, mod(xp))
    hn = jnp.where(is_last, 0.0, mod(xn))
    h_ext = jnp.concatenate([hp, mod(x), hn], axis=0).astype(BF16)
    n_ext = TM + 2 * FFN_HALO
    ug = _dot(h_ext, wup_ref[:, 0:FFN_HIDDEN])
    uv = _dot(mod(x).astype(BF16), wup_ref[:, FFN_HIDDEN:])
    lo = pltpu.roll(ug, 1, axis=0)[FFN_HALO:FFN_HALO + TM]
    hi = pltpu.roll(ug, n_ext - 1, axis=0)[FFN_HALO:FFN_HALO + TM]
    conv = (lo * dww_ref[0:1, :] + ug[FFN_HALO:FFN_HALO + TM] * dww_ref[1:2, :] + hi * dww_ref[2:3, :]
            + dwb_ref[...])
    act = (_silu(conv) * uv).astype(BF16)
    y = _dot(act, wdn_ref[...])
    out = _layer_norm(DEEPNORM_ALPHA * x + gate_mod * y, lng_ref[...], lnb_ref[...])
    marks = [ug[0:SUBLANES, c:c + LANES] for c in range(0, FFN_HIDDEN, MXU_N)]
    return out, marks


def _ring_tiles(s, ring_ref, lag):
    cur = (s + 2 * RING - lag) % RING
    prv = (s + 2 * RING - lag - 1) % RING
    nxt = (s + 2 * RING - lag + 1) % RING
    return ring_ref[cur], ring_ref[prv, TM - FFN_HALO:TM, :], ring_ref[nxt, 0:FFN_HALO, :]


def _lagged_edges(s, lag, n_tiles, n_ctx_tiles, n_total):
    t = jnp.clip(s - lag, 0, n_total - 1)
    return _segment_edges(t % n_tiles, n_tiles, n_ctx_tiles)


def _conf_glu(x, xp, xn, mod_ref, is_first, is_last, w1_ref, b1_ref):
    shift = mod_ref[0, 0, 0:1, :]
    scale = mod_ref[0, 0, 1:2, :]
    n_ext = TM + 2 * CONF_HALO
    x_ext = jnp.concatenate([xp, x, xn], axis=0)
    h_ext = (x_ext * (1.0 + scale) + shift).astype(BF16)
    a = _dot(h_ext, w1_ref[...]) + b1_ref[...]
    a = a[:, :D_MODEL] * _sigmoid(a[:, D_MODEL:])
    rows = lax.broadcasted_iota(jnp.int32, (n_ext, 1), 0)
    pad = ((rows < CONF_HALO) & is_first) | ((rows >= CONF_HALO + TM) & is_last)
    return jnp.where(pad, 0.0, a)


def _conf_dwconv(a_ref, ia, c_ref, ic, dww_ref, dwb_ref, marks, zero_ref):
    half = (CONF_KERNEL - 1) // 2
    base = CONF_HALO - half
    n_slab = CONF_ROWS + 2 * CONF_HALO

    for cb in range(D_MODEL // LANES):
        lanes = slice(cb * LANES, (cb + 1) * LANES)
        for rb in range(TM // CONF_ROWS):
            slab = a_ref[ia, rb * CONF_ROWS:rb * CONF_ROWS + n_slab, lanes]
            mark = marks[(cb * (TM // CONF_ROWS) + rb) % len(marks)]
            edge = pltpu.bitcast(pltpu.bitcast(mark, jnp.int32) & zero_ref[...], F32)
            acc = jnp.broadcast_to(dwb_ref[:, lanes] + edge[0:1, :], (CONF_ROWS, LANES))
            for r in range(SUBLANES):
                sh = pltpu.roll(slab, n_slab - r, axis=0) if r else slab
                for mm in range(-(-(base + CONF_KERNEL) // SUBLANES)):
                    kk = SUBLANES * mm + r - base
                    if 0 <= kk < CONF_KERNEL:
                        acc = acc + sh[SUBLANES * mm:SUBLANES * mm + CONF_ROWS] * dww_ref[kk:kk + 1, lanes]
            c_ref[ic, rb * CONF_ROWS:(rb + 1) * CONF_ROWS, lanes] = acc


def _conf_out(conv, x, mod_ref, cg_ref, cb_ref, w2_ref, b2_ref, lng_ref, lnb_ref):
    gate_mod = mod_ref[0, 0, 2:3, :]
    z = _silu(_layer_norm(conv, cg_ref[...], cb_ref[...])).astype(BF16)
    y = _dot(z, w2_ref[...]) + b2_ref[...]
    return _layer_norm(DEEPNORM_ALPHA * x + gate_mod * y, lng_ref[...], lnb_ref[...])


def _zero_once(s, *refs):
    @pl.when(s == 0)
    def _():
        for ref in refs:
            ref[...] = jnp.zeros(ref.shape, F32)


def _conf_ffn_kernel(x_ref, xp_ref, xn_ref, xres_ref, moda_ref, modc_ref, zero_ref, modf_ref,
                     w1_ref, b1_ref, cdww_ref, cdwb_ref, cg_ref, cb_ref, w2_ref, b2_ref, tlng_ref, tlnb_ref,
                     wup_ref, fdww_ref, fdwb_ref, wdn_ref, flng_ref, flnb_ref,
                     o_ref, ring_ref, a_ref, c_ref, *, n_tiles, n_ctx_tiles, n_total):
    s = pl.program_id(0)
    _zero_once(s, ring_ref, a_ref, c_ref)
    edges = functools.partial(_lagged_edges, s, n_tiles=n_tiles, n_ctx_tiles=n_ctx_tiles, n_total=n_total)
    ffn_in = _ring_tiles(s, ring_ref, CONF_FFN_LAG)
    ring_ref[(s + 2) % RING] = _conf_out(c_ref[s % 2], xres_ref[0], modc_ref, cg_ref, cb_ref, w2_ref, b2_ref,
                                         tlng_ref, tlnb_ref)
    o_ref[0], marks = _ffn_tile(*ffn_in, modf_ref, *edges(CONF_FFN_LAG),
                                wup_ref, fdww_ref, fdwb_ref, wdn_ref, flng_ref, flnb_ref)
    _conf_dwconv(a_ref, (s + 1) % 2, c_ref, (s + 1) % 2, cdww_ref, cdwb_ref, marks, zero_ref)
    a_ref[s % 2] = _conf_glu(x_ref[0], xp_ref[0], xn_ref[0], moda_ref, *edges(0), w1_ref, b1_ref)


def _mixout_ffn_kernel(att_ref, hg_ref, *refs, n_streams, n_tiles, n_ctx_tiles, n_total):
    x_refs, refs = refs[:n_streams], refs[n_streams:]
    (modt_ref, modf_ref, wo_ref, tlng_ref, tlnb_ref, wup_ref, fdww_ref, fdwb_ref, wdn_ref, flng_ref, flnb_ref,
     o_ref, ring_ref) = refs
    s = pl.program_id(0)
    _zero_once(s, ring_ref)
    ffn_in = _ring_tiles(s, ring_ref, MIXER_FFN_LAG)
    y = _dot(att_ref[0], wo_ref[0:ATT_Q_W, :]) + _dot(hg_ref[0], wo_ref[ATT_Q_W:, :])
    o_ref[0], _ = _ffn_tile(*ffn_in, modf_ref, *_lagged_edges(s, MIXER_FFN_LAG, n_tiles, n_ctx_tiles, n_total),
                            wup_ref, fdww_ref, fdwb_ref, wdn_ref, flng_ref, flnb_ref)
    gate = modt_ref[0, 0, 2:3, :]
    x = _stream_tile(x_refs, jnp.minimum(s, n_total - 1) % n_tiles, n_ctx_tiles)
    ring_ref[s % RING] = _layer_norm(DEEPNORM_ALPHA * x + gate * y, tlng_ref[...], tlnb_ref[...])


def _tail_ffn(kind, acts, mods, tail_params, ffn_params, n_in_ctx_tiles, skip_ctx):
    B, S, _ = acts[0].shape
    off = n_in_ctx_tiles if skip_ctx else 0
    nt = S // TM - off
    nct = 0 if skip_ctx else n_in_ctx_tiles
    n_total = B * nt
    ffn_lag = CONF_FFN_LAG if kind == "conformer" else MIXER_FFN_LAG

    def lagged(lag):
        def tile(s):
            t = jnp.clip(s - lag, 0, n_total - 1)
            return t // nt, t % nt
        return tile

    tail_tile = lagged(0)

    def rows(w, tile_fn=tail_tile):
        def index(s):
            b, i = tile_fn(s)
            return (b, i + off, 0)
        return pl.BlockSpec((1, TM, w), index)

    def halo(before):
        per = TM // CONF_HALO

        def index(s):
            b, i = tail_tile(s)
            blk = (i + off) * per - 1 if before else (i + off + 1) * per
            return (b, jnp.clip(blk, 0, (S // TM) * per - 1), 0)
        return pl.BlockSpec((1, CONF_HALO, D_MODEL), index)

    def mod_spec(tile_fn):
        def index(s):
            b, i = tile_fn(s)
            return (b, jnp.where(i < nct, 0, 1), 0, 0)
        return pl.BlockSpec((1, 1, 6, D_MODEL), index)

    def out_index(s):
        b, i = lagged(ffn_lag)(s)
        return (b, i, 0)

    small = lambda a: pl.BlockSpec(a.shape, lambda s: (0,) * a.ndim)
    spec_of = lambda a: _resident(a.shape) if a.size * a.dtype.itemsize > (1 << 20) else small(a)
    scratch = [pltpu.VMEM((RING, TM, D_MODEL), F32)]
    if kind == "conformer":
        xs, = acts
        body = _conf_ffn_kernel
        zero = jnp.zeros((SUBLANES, LANES), jnp.int32)
        in_specs = [rows(D_MODEL), halo(True), halo(False), rows(D_MODEL, lagged(2)),
                    mod_spec(tail_tile), mod_spec(lagged(2)), small(zero)]
        operands = [xs, xs, xs, xs, mods, mods, zero]
        scratch += [pltpu.VMEM((2, TM + 2 * CONF_HALO, D_MODEL), F32), pltpu.VMEM((2, TM, D_MODEL), F32)]
    else:
        streams = acts[2:]
        assert len(streams) == 1 or not skip_ctx
        body = functools.partial(_mixout_ffn_kernel, n_streams=len(streams))
        in_specs = [rows(a.shape[2]) for a in acts[:2]]
        in_specs += [rows(D_MODEL)] if len(streams) == 1 else _stream_specs(streams, nct, tail_tile)
        in_specs += [mod_spec(tail_tile)]
        operands = list(acts) + [mods]
    in_specs += [mod_spec(lagged(ffn_lag))] + [spec_of(a) for a in tail_params + ffn_params]
    operands += [mods] + list(tail_params) + list(ffn_params)
    return pl.pallas_call(
        functools.partial(body, n_tiles=nt, n_ctx_tiles=nct, n_total=n_total),
        grid=(n_total + ffn_lag,),
        in_specs=in_specs,
        out_specs=pl.BlockSpec((1, TM, D_MODEL), out_index),
        out_shape=jax.ShapeDtypeStruct((B, nt * TM, D_MODEL), F32),
        scratch_shapes=scratch,
        compiler_params=_cparams(1),
        name=kind + "_ffn",
    )(*operands)


def _rope_tables(n_ctx, n_lat):
    rows = n_lat // GRID_W
    row = jnp.repeat(jnp.arange(rows), GRID_W).astype(F32)
    col = jnp.tile(jnp.arange(GRID_W), rows).astype(F32)
    quarter = ATT_HEAD_DIM // 4
    inv_freq = ROPE_BASE ** (-jnp.arange(quarter, dtype=F32) / quarter)
    ang_r = row[:, None] * inv_freq
    ang_c = col[:, None] * inv_freq
    ang = jnp.concatenate([ang_r, ang_r, ang_c, ang_c], axis=-1)
    ang = jnp.concatenate([jnp.zeros((n_ctx, ATT_HEAD_DIM), F32), ang], axis=0)
    ang = jnp.concatenate([ang, ang], axis=-1)
    cos, sin = jnp.cos(ang), jnp.sin(ang)
    low = (jnp.arange(LANES) % (2 * quarter)) < quarter
    return cos, jnp.where(low, -sin, 0.0), jnp.where(low, 0.0, sin)


def _chunk_constants():
    t = jnp.arange(CHUNK)
    full = lambda col: jnp.broadcast_to(col[:, None], (CHUNK, HG_DIM)).astype(F32)
    sgn, mask = [], []
    for d in range(2):
        sgn_d, mask_d = [], []
        for lv in range(CHUNK_LEVELS):
            is_q = ((t >> lv) & 1) == (1 - d)
            same = (t[:, None] >> (lv + 1)) == (t[None, :] >> (lv + 1))
            sgn_d.append(full(jnp.where(is_q, 1.0, -1.0)))
            mask_d.append((same & is_q[:, None] & ~is_q[None, :]).astype(F32))
        mask_d.append((t[:, None] == t[None, :]).astype(F32))
        sgn.append(jnp.stack(sgn_d))
        mask.append(jnp.stack(mask_d))
    odd = (t & 1) == 1
    r = t & 3
    coef = jnp.stack([
        jnp.stack([full(odd), full(~odd), full(r == 3), full(r == 2), full(r == 1), full(r == 0)]),
        jnp.stack([full(~odd), full(odd), full(r == 0), full(r == 3), full(r == 2), full(r == 1)]),
    ])
    return jnp.stack(sgn), jnp.stack(mask), coef


def _prefix_matrices():
    t = jnp.arange(CHUNK)
    lower = t[None, :] <= t[:, None]
    return jnp.stack([lower, lower.T]).astype(BF16)


def kernel(x, c, ctx, c_ctx, mod_w, mod_b, post_ln_g, post_ln_b, mix_w_in, mix_w_out, att_sink, hg_lb_logits, hg_norm_g, conf_pw1_w, conf_pw1_b, conf_dw_w, conf_dw_b, conf_ln_g, conf_ln_b, conf_pw2_w, conf_pw2_b, ffn_w_up, ffn_dw_w, ffn_dw_b, ffn_w_down):
    B, T, D = x.shape
    L = ctx.shape[1]
    assert D == D_MODEL and L % TM == 0 and T % TM == 0 and L >= ATT_BLOCK and T % GRID_W == 0
    n_ctx_tiles = L // TM

    n_cond = -(-(B + 1) // SUBLANES) * SUBLANES
    cond = jnp.zeros((n_cond, D), F32).at[:B].set(c).at[B].set(c_ctx)
    mod = _modulation(cond, mod_w.astype(BF16), mod_b[:, None, :])
    mod_x = mod[:, :B].reshape(DEPTH, B, 1, 6, D)
    mod_c = jnp.broadcast_to(mod[:, B].reshape(DEPTH, 1, 1, 6, D), (DEPTH, B, 1, 6, D))
    mods = jnp.concatenate([mod_c, mod_x], axis=2)

    cos, sa, sb = _rope_tables(L, T)
    chunk_consts = _chunk_constants()
    tri = _prefix_matrices()
    vec = lambda a: a.reshape(1, -1)

    streams = (ctx, x)
    has_ctx = True
    for layer in range(DEPTH):
        m = layer // 2
        need_ctx_out = any(j % 2 == 0 for j in range(layer + 1, DEPTH))
        nct = n_ctx_tiles if has_ctx else 0
        lg, lb = post_ln_g[layer], post_ln_b[layer]
        ffn_params = (ffn_w_up[layer].astype(BF16), ffn_dw_w[layer], vec(ffn_dw_b[layer]),
                      ffn_w_down[layer].astype(BF16), vec(lg[1]), vec(lb[1]))
        if layer % 2 == 0:
            q, k, v, hq, bf, kf, bb, kb, hi, hgate = _mix_in(
                streams, mods[layer], mix_w_in[m].astype(BF16), cos, sa, sb, hg_lb_logits, tri, m, nct)
            att = _attention(att_sink[m], q, k, v, L, need_ctx_out)
            hg = _hgrn(hq, bf, kf, bb, kb, hi, hgate, vec(hg_norm_g[m]), chunk_consts, L)
            tail_params = (mix_w_out[m].astype(BF16), vec(lg[0]), vec(lb[0]))
            xs = _tail_ffn("mixer", (att, hg) + streams, mods[layer], tail_params, ffn_params, nct,
                           not need_ctx_out)
        else:
            tail_params = (conf_pw1_w[m].astype(BF16), vec(conf_pw1_b[m]), conf_dw_w[m], vec(conf_dw_b[m]),
                           vec(conf_ln_g[m]), vec(conf_ln_b[m]), conf_pw2_w[m].astype(BF16), vec(conf_pw2_b[m]),
                           vec(lg[0]), vec(lb[0]))
            xs = _tail_ffn("conformer", streams, mods[layer], tail_params, ffn_params, nct,
                           has_ctx and not need_ctx_out)
        streams = (xs,)
        has_ctx = has_ctx and need_ctx_out
    return xs[:, L:] if has_ctx else xs
```

```python
import functools

import jax
import jax.numpy as jnp
from jax import lax
from jax.experimental import pallas as pl
from jax.experimental.pallas import tpu as pltpu

F32 = jnp.float32
BF16 = jnp.bfloat16

D_MODEL = 1024
DEPTH = 4
GRID_W = 64
ATT_HEADS = 8
ATT_KV_HEADS = 2
ATT_HEAD_DIM = 64
ATT_WINDOW = 128
ATT_BLOCK = 128
ROPE_BASE = 10000.0
HG_HEADS = 4
HG_DIM = 128
CONF_KERNEL = 31
FFN_HIDDEN = 2816
DEEPNORM_ALPHA = (2 * DEPTH) ** 0.25
LN_EPS = 1e-5
RMS_EPS = 1e-6
MASK_VALUE = -1e30
LB_FLOOR = 1e-30

ATT_Q_W = ATT_HEADS * ATT_HEAD_DIM
ATT_KV_W = ATT_KV_HEADS * ATT_HEAD_DIM
HG_W = HG_HEADS * HG_DIM
OFF_AQ = 0
OFF_AK = OFF_AQ + ATT_Q_W
OFF_AV = OFF_AK + ATT_KV_W
OFF_HQ = OFF_AV + ATT_KV_W
OFF_FF = OFF_HQ + HG_W
OFF_FB = OFF_FF + HG_W
OFF_HI = OFF_FB + HG_W
OFF_HGATE = OFF_HI + HG_W
MIX_IN_W = OFF_HGATE + HG_W

LANES = 128
SUBLANES = 8
MXU_N = 256
TM = 256
CHUNK = 128
CHUNK_LEVELS = 7
READOUT_UNROLL = 6
FFN_HALO = SUBLANES
CONF_HALO = 16
VMEM_LIMIT = 56 * 1024 * 1024


def _cparams(n_grid):
    return pltpu.CompilerParams(
        dimension_semantics=("arbitrary",) * n_grid, vmem_limit_bytes=VMEM_LIMIT)


def _dot(a, b):
    return jnp.dot(a, b, preferred_element_type=F32)


def _dot_nt(a, b):
    return lax.dot_general(a, b, (((1,), (1,)), ((), ())), preferred_element_type=F32)


def _dot_tn(a, b):
    return lax.dot_general(a, b, (((0,), (0,)), ((), ())), preferred_element_type=F32)


def _sigmoid(x):
    return 0.5 * jnp.tanh(0.5 * x) + 0.5


def _silu(x):
    h = 0.5 * x
    return h * (1.0 + jnp.tanh(h))


def _layer_norm(z, g, b):
    mu = jnp.mean(z, axis=-1, keepdims=True)
    zc = z - mu
    var = jnp.mean(zc * zc, axis=-1, keepdims=True)
    return zc * lax.rsqrt(var + LN_EPS) * g + b


def _resident(shape):
    nd = len(shape)
    return pl.BlockSpec(shape, lambda *_: (0,) * nd, pipeline_mode=pl.Buffered(1))


MOD_TN = 1536


def _mod_kernel(cond_ref, w_ref, b_ref, o_ref):
    s = _silu(cond_ref[...])
    hi = s.astype(BF16)
    lo = (s - hi.astype(F32)).astype(BF16)
    w = w_ref[0]
    o_ref[0] = _dot(hi, w) + _dot(lo, w) + b_ref[0]


def _modulation(cond, mod_w, mod_b):
    R = cond.shape[0]
    n6 = mod_w.shape[2]
    return pl.pallas_call(
        _mod_kernel,
        grid=(DEPTH, n6 // MOD_TN),
        in_specs=[
            pl.BlockSpec((R, D_MODEL), lambda l, j: (0, 0)),
            pl.BlockSpec((1, D_MODEL, MOD_TN), lambda l, j: (l, 0, j)),
            pl.BlockSpec((1, 1, MOD_TN), lambda l, j: (l, 0, j)),
        ],
        out_specs=pl.BlockSpec((1, R, MOD_TN), lambda l, j: (l, 0, j)),
        out_shape=jax.ShapeDtypeStruct((DEPTH, R, n6), F32),
        compiler_params=_cparams(2),
        name="modulation",
    )(cond, mod_w, mod_b)


def _stream_tile(x_refs, i, n_ctx_tiles):
    if len(x_refs) == 1:
        return x_refs[0][0]
    return jnp.where(i < n_ctx_tiles, x_refs[0][0], x_refs[1][0])


def _stream_specs(streams, n_ctx_tiles, tile_fn):
    if len(streams) == 1:
        def index(*g):
            b, i = tile_fn(*g)
            return (b, i, 0)
        return [pl.BlockSpec((1, TM, D_MODEL), index)]

    def ctx_index(*g):
        b, i = tile_fn(*g)
        return (b, jnp.minimum(i, n_ctx_tiles - 1), 0)

    def lat_index(*g):
        b, i = tile_fn(*g)
        return (b, jnp.maximum(i - n_ctx_tiles, 0), 0)
    return [pl.BlockSpec((1, TM, D_MODEL), ctx_index), pl.BlockSpec((1, TM, D_MODEL), lat_index)]


def _mix_in_kernel(*refs, layer_m, n_streams, n_ctx_tiles):
    x_refs, refs = refs[:n_streams], refs[n_streams:]
    (mod_ref, w_ref, cos_ref, sa_ref, sb_ref, lbl_ref, tri_ref,
     q_ref, k_ref, v_ref, hq_ref, bf_ref, kf_ref, bb_ref, kb_ref, hi_ref, hg_ref) = refs
    x = _stream_tile(x_refs, pl.program_id(1), n_ctx_tiles)
    shift = mod_ref[0, 0, 0:1, :]
    scale = mod_ref[0, 0, 1:2, :]
    h = (x * (1.0 + scale) + shift).astype(BF16)

    def proj(off, width):
        return _dot(h, w_ref[:, off:off + width])

    fr_fwd = proj(OFF_FF, HG_W)
    fr_bwd = proj(OFF_FB, HG_W)

    cos = cos_ref[...]
    sa = sa_ref[...]
    sb = sb_ref[...]

    def rope(a, reps):
        w = a.shape[1]
        c = jnp.concatenate([cos] * reps, axis=1) if reps > 1 else cos
        s1 = jnp.concatenate([sa] * reps, axis=1) if reps > 1 else sa
        s2 = jnp.concatenate([sb] * reps, axis=1) if reps > 1 else sb
        up = pltpu.roll(a, w - 16, axis=1)
        dn = pltpu.roll(a, 16, axis=1)
        return a * c + up * s1 + dn * s2

    q = rope(proj(OFF_AQ, ATT_Q_W), ATT_Q_W // LANES) * (ATT_HEAD_DIM ** -0.5)
    q_ref[0] = q.astype(BF16)

    lane = lax.broadcasted_iota(jnp.int32, (TM, LANES), 1)
    first = lane < ATT_HEAD_DIM

    def pair_rep(a):
        sw = pltpu.roll(a, ATT_HEAD_DIM, axis=1)
        return jnp.concatenate([jnp.where(first, a, sw), jnp.where(first, sw, a)], axis=1)

    k_ref[0] = pair_rep(rope(proj(OFF_AK, ATT_KV_W), 1)).astype(BF16)
    v_ref[0] = pair_rep(proj(OFF_AV, ATT_KV_W)).astype(BF16)

    logits = lbl_ref[...]
    n_mix = logits.shape[0]
    mx = logits[0:1, :]
    for r in range(1, n_mix):
        mx = jnp.maximum(mx, logits[r:r + 1, :])
    ex = [jnp.exp(logits[r:r + 1, :] - mx) for r in range(n_mix)]
    tot = ex[0]
    for r in range(1, n_mix):
        tot = tot + ex[r]
    cum = ex[0] / tot
    p0 = cum
    for r in range(1, layer_m + 1):
        cum = cum + ex[r] / tot
    lb = cum - p0
    lb_floor = jnp.maximum(lb, LB_FLOOR)
    one_m = 1.0 - lb

    def gates(fr, d, b_out, k_out):
        e = jnp.exp(-jnp.abs(fr))
        r = 1.0 / (1.0 + e)
        er = e * r
        pos = fr >= 0
        k_out[0] = one_m * jnp.where(pos, er, r)
        lf = jnp.log2(lb_floor + one_m * jnp.where(pos, r, er))
        tri = tri_ref[d]
        for c in range(TM // CHUNK):
            g = lf[c * CHUNK:(c + 1) * CHUNK]
            g_hi = g.astype(BF16)
            g_lo = (g - g_hi.astype(F32)).astype(BF16)
            b_out[0, c * CHUNK:(c + 1) * CHUNK, :] = _dot(tri, g_hi) + _dot(tri, g_lo)

    hq_ref[0] = proj(OFF_HQ, HG_W)
    gates(fr_fwd, 0, bf_ref, kf_ref)
    hi_ref[0] = proj(OFF_HI, HG_W).astype(BF16)
    gates(fr_bwd, 1, bb_ref, kb_ref)
    hg_ref[0] = proj(OFF_HGATE, HG_W)


def _mix_in(streams, mods, w_in, cos, sa, sb, lb_logits, tri, layer_m, n_ctx_tiles):
    B = streams[0].shape[0]
    S = sum(a.shape[1] for a in streams)
    nt = S // TM
    row = lambda w: pl.BlockSpec((1, TM, w), lambda b, i: (b, i, 0))
    tab = pl.BlockSpec((TM, LANES), lambda b, i: (i, 0))
    sds = lambda w, dt: jax.ShapeDtypeStruct((B, S, w), dt)
    return pl.pallas_call(
        functools.partial(_mix_in_kernel, layer_m=layer_m, n_streams=len(streams), n_ctx_tiles=n_ctx_tiles),
        grid=(B, nt),
        in_specs=_stream_specs(streams, n_ctx_tiles, lambda b, i: (b, i)) + [
            pl.BlockSpec((1, 1, 6, D_MODEL), lambda b, i: (b, jnp.where(i < n_ctx_tiles, 0, 1), 0, 0)),
            _resident((D_MODEL, MIX_IN_W)),
            tab, tab, tab,
            _resident(lb_logits.shape),
            _resident(tri.shape),
        ],
        out_specs=[row(ATT_Q_W), row(2 * LANES), row(2 * LANES), row(HG_W), row(HG_W), row(HG_W),
                   row(HG_W), row(HG_W), row(HG_W), row(HG_W)],
        out_shape=[sds(ATT_Q_W, BF16), sds(2 * LANES, BF16), sds(2 * LANES, BF16), sds(HG_W, F32),
                   sds(HG_W, F32), sds(HG_W, F32), sds(HG_W, F32), sds(HG_W, F32), sds(HG_W, BF16),
                   sds(HG_W, F32)],
        compiler_params=_cparams(2),
        name="mix_in",
    )(*streams, mods, w_in, cos, sa, sb, lb_logits, tri)


GROUP = ATT_HEADS // ATT_KV_HEADS


def _attn_kernel(sink_ref, q_ref, k_ref, v_ref, o_ref, p_ref, rden_ref, *, n_ctx, n_lat, ctx_out):
    hkv = pl.program_id(1)
    QB = ATT_BLOCK
    nb = n_lat // QB
    lane = lax.broadcasted_iota(jnp.int32, (QB, LANES), 1)
    first = lane < ATT_HEAD_DIM
    rows4 = lax.broadcasted_iota(jnp.int32, (GROUP * QB, LANES), 0)
    rq = rows4 & (QB - 1)
    col = lax.broadcasted_iota(jnp.int32, (GROUP * QB, LANES), 1)
    grp = lax.broadcasted_iota(jnp.int32, (GROUP * QB, 1), 0) // QB
    sink = jnp.zeros((GROUP * QB, 1), F32)
    for g in range(GROUP):
        sink = jnp.where(grp == g, sink_ref[hkv * GROUP + g], sink)

    def stack_q(q):
        qa, qb = q[:, :LANES], q[:, LANES:]
        z = jnp.zeros_like(qa)
        return jnp.concatenate([jnp.where(first, qa, z), jnp.where(first, z, qa),
                                jnp.where(first, qb, z), jnp.where(first, z, qb)], axis=0)

    def unstack_o(o):
        return jnp.concatenate([jnp.where(first, o[0:QB], o[QB:2 * QB]),
                                jnp.where(first, o[2 * QB:3 * QB], o[3 * QB:4 * QB])], axis=1)

    kc = k_ref[0, 0:n_ctx, :]
    vc = v_ref[0, 0:n_ctx, :]
    tiles = lambda a: [a[:, c:c + LANES] for c in range(0, a.shape[1], LANES)]

    def softmax_parts(parts):
        mt = None
        for s in parts:
            for t in tiles(s):
                mt = t if mt is None else jnp.maximum(mt, t)
        m = jnp.maximum(sink, jnp.max(mt, axis=-1, keepdims=True))
        dt = None
        ps = []
        for s in parts:
            p = jnp.exp(s - m)
            for t in tiles(p):
                dt = t if dt is None else dt + t
            ps.append(p.astype(BF16))
        den = jnp.exp(sink - m) + jnp.sum(dt, axis=-1, keepdims=True)
        return ps, 1.0 / den

    def block_rows(i):
        r0 = pl.multiple_of(n_ctx + i * QB, QB)
        rp = pl.multiple_of(r0 - QB, QB)
        rn = pl.multiple_of(jnp.minimum(r0 + QB, n_ctx + n_lat - QB), QB)
        return rp, r0, rn

    def qk_scores(i):
        rp, r0, rn = block_rows(i)
        q4 = stack_q(q_ref[0, pl.ds(r0, QB), :])
        lo_col = jnp.where(i > 0, rq, LANES)
        hi_col = jnp.where(i < nb - 1, rq, -1)
        sp = jnp.where(col >= lo_col, _dot_nt(q4, k_ref[0, pl.ds(rp, QB), :]), MASK_VALUE)
        ss = _dot_nt(q4, k_ref[0, pl.ds(r0, QB), :])
        sn = jnp.where(col <= hi_col, _dot_nt(q4, k_ref[0, pl.ds(rn, QB), :]), MASK_VALUE)
        return [sp, ss, sn, _dot_nt(q4, kc)]

    def store_probs(parts, slot):
        ps, rden = softmax_parts(parts)
        p_ref[slot] = jnp.concatenate(ps, axis=1)
        rden_ref[slot] = jnp.broadcast_to(rden, (GROUP * QB, LANES))

    def output_stage(i, slot):
        rp, r0, rn = block_rows(i)
        acc = (_dot(p_ref[slot, :, 0:QB], v_ref[0, pl.ds(rp, QB), :])
               + _dot(p_ref[slot, :, QB:2 * QB], v_ref[0, pl.ds(r0, QB), :])
               + _dot(p_ref[slot, :, 2 * QB:3 * QB], v_ref[0, pl.ds(rn, QB), :])
               + _dot(p_ref[slot, :, 3 * QB:], vc))
        o_ref[0, pl.ds(r0, QB), :] = unstack_o(acc * rden_ref[slot]).astype(BF16)

    p_ref[...] = jnp.zeros(p_ref.shape, BF16)
    rden_ref[...] = jnp.zeros(rden_ref.shape, F32)

    def trip(t, carry):
        i0 = 2 * t
        s_a = qk_scores(i0)
        output_stage(jnp.maximum(i0 - 1, 0), 1)
        s_b = qk_scores(i0 + 1)
        store_probs(s_a, 0)
        output_stage(i0, 0)
        store_probs(s_b, 1)
        return carry

    lax.fori_loop(0, nb // 2, trip, 0)
    output_stage(nb - 1, 1)

    for j in range(n_ctx // QB):
        if ctx_out:
            q4 = stack_q(q_ref[0, j * QB:(j + 1) * QB, :])
            (p,), rden = softmax_parts([_dot_nt(q4, kc)])
            o_ref[0, j * QB:(j + 1) * QB, :] = unstack_o(_dot(p, vc) * rden).astype(BF16)
        else:
            o_ref[0, j * QB:(j + 1) * QB, :] = jnp.zeros((QB, 2 * LANES), BF16)


def _attention(sink, q, k, v, n_ctx, ctx_out):
    B, S, _ = q.shape
    return pl.pallas_call(
        functools.partial(_attn_kernel, n_ctx=n_ctx, n_lat=S - n_ctx, ctx_out=ctx_out),
        grid=(B, ATT_KV_HEADS),
        in_specs=[
            pl.BlockSpec(memory_space=pltpu.SMEM),
            pl.BlockSpec((1, S, 2 * LANES), lambda b, h: (b, 0, h)),
            pl.BlockSpec((1, S, LANES), lambda b, h: (b, 0, h)),
            pl.BlockSpec((1, S, LANES), lambda b, h: (b, 0, h)),
        ],
        out_specs=pl.BlockSpec((1, S, 2 * LANES), lambda b, h: (b, 0, h)),
        out_shape=jax.ShapeDtypeStruct((B, S, ATT_Q_W), BF16),
        scratch_shapes=[pltpu.VMEM((2, GROUP * ATT_BLOCK, 3 * ATT_BLOCK + n_ctx), BF16),
                        pltpu.VMEM((2, GROUP * ATT_BLOCK, LANES), F32)],
        compiler_params=_cparams(2),
        name="attention",
    )(sink, q, k, v)


def _hgrn_scores(q, b, k, v, d, slot, st_ref, sgn_ref, mask_ref, coef_ref, sc_ref, qh_ref, sti_ref):
    C = CHUNK
    SUB = SUBLANES
    rev = d == 1
    b_p1 = pltpu.roll(b, 1, axis=0)
    b_p2 = pltpu.roll(b, 2, axis=0)
    b_n1 = pltpu.roll(b, C - 1, axis=0)
    b_n2 = pltpu.roll(b, C - 2, axis=0)
    rows_of = lambda a, i, n: a[i * n:(i + 1) * n]
    sc = [None] * (C // SUB)

    def add_rows(first_row, p):
        for i in range(p.shape[0] // SUB):
            j = first_row // SUB + i
            blk = p[i * SUB:(i + 1) * SUB]
            sc[j] = blk if sc[j] is None else sc[j] + blk

    for lv in range(CHUNK_LEVELS):
        half = 1 << lv
        if half < SUB:
            sgn = sgn_ref[d, lv]
            if lv == 0:
                bm = (b_n1 if rev else b_p1) * coef_ref[d, 0] + b * coef_ref[d, 1]
            elif lv == 1:
                far = b_n2 if rev else b_p2
                bm = far * coef_ref[d, 2] + b_p1 * coef_ref[d, 3] + b * coef_ref[d, 4] + b_n1 * coef_ref[d, 5]
            else:
                nblk = C // (2 * half)
                b3 = b.reshape(nblk, 2 * half, HG_DIM)
                ref_row = half if rev else half - 1
                bm = jnp.broadcast_to(b3[:, ref_row:ref_row + 1, :], (nblk, 2 * half, HG_DIM)).reshape(C, HG_DIM)
            x = (jnp.where(sgn > 0, q, k) * jnp.exp2((b - bm) * sgn)).astype(BF16)
            add_rows(0, _dot_nt(x, x) * mask_ref[d, lv])
        else:
            xs, xq, q_first = [], [], []
            for blk in range(C // half):
                is_q = (blk % 2 == 0) if rev else (blk % 2 == 1)
                pair0 = (blk // 2) * 2 * half
                ref = pair0 + (half if rev else half - 1)
                bb = rows_of(b, blk, half)
                bm = b[ref:ref + 1, :]
                xb = ((rows_of(q, blk, half) * jnp.exp2(bb - bm)) if is_q
                      else (rows_of(k, blk, half) * jnp.exp2(bm - bb))).astype(BF16)
                xs.append(xb)
                if is_q:
                    xq.append(xb)
                    q_first.append(blk * half)
            p = _dot_nt(jnp.concatenate(xq, axis=0), jnp.concatenate(xs, axis=0))
            for i, r0 in enumerate(q_first):
                add_rows(r0, p[i * half:(i + 1) * half] * mask_ref[d, lv, r0:r0 + half, :])
    diag = jnp.sum(q * k, axis=-1, keepdims=True)
    scores = jnp.concatenate(sc, axis=0) + diag * mask_ref[d, CHUNK_LEVELS]
    b_end = b[0:1, :] if rev else b[C - 1:C, :]
    st = st_ref[d]
    sc_ref[d, slot] = scores.astype(BF16)
    qh_ref[d, slot] = (q * jnp.exp2(b)).astype(BF16)
    sti_ref[d, slot] = st.astype(BF16)
    kh = (k * jnp.exp2(b_end - b)).astype(BF16)
    st_ref[d] = st * jnp.exp2(b_end) + _dot_tn(v, kh)


def _hgrn_output(v, d, slot, sc_ref, qh_ref, sti_ref):
    return _dot(sc_ref[d, slot], v) + _dot_nt(qh_ref[d, slot], sti_ref[d, slot])


def _hgrn_kernel(q_ref, bf_ref, kf_ref, bb_ref, kb_ref, v_ref, g_ref, ng_ref, sgn_ref, mask_ref,
                 coef_ref, o_ref, of_ref, ob_ref, st_ref, sc_ref, qh_ref, sti_ref, *, n_ctx_chunks, n_lat_chunks):
    nc, nl = n_ctx_chunks, n_lat_chunks
    n = nc + nl
    st_ref[...] = jnp.zeros(st_ref.shape, F32)
    sc_ref[...] = jnp.zeros(sc_ref.shape, BF16)
    qh_ref[...] = jnp.zeros(qh_ref.shape, BF16)
    sti_ref[...] = jnp.zeros(sti_ref.shape, BF16)
    b_refs = (bf_ref, bb_ref)
    k_refs = (kf_ref, kb_ref)
    o_refs = (of_ref, ob_ref)
    consts = (st_ref, sgn_ref, mask_ref, coef_ref, sc_ref, qh_ref, sti_ref)

    def rows_of_step(j, d):
        c = j if d == 0 else jnp.where(j < nc, nc - 1 - j, 2 * nc + nl - 1 - j)
        return pl.ds(pl.multiple_of(c * CHUNK, CHUNK), CHUNK)

    def first_half(j, slot):
        for d in range(2):
            r = rows_of_step(j, d)
            _hgrn_scores(q_ref[0, r, :], b_refs[d][0, r, :], k_refs[d][0, r, :], v_ref[0, r, :], d, slot, *consts)

    def second_half(j, slot):
        for d in range(2):
            r = rows_of_step(j, d)
            o_refs[d][r, :] = _hgrn_output(v_ref[0, r, :], d, slot, sc_ref, qh_ref, sti_ref)

    def trip(i, carry):
        j0 = 2 * i
        first_half(j0, 0)
        second_half(jnp.maximum(j0 - 1, 0), 1)
        first_half(j0 + 1, 1)
        second_half(j0, 0)
        return carry

    lax.fori_loop(0, n // 2, trip, 0, unroll=True)
    second_half(n - 1, 1)

    def readout(j, carry):
        rows = pl.ds(pl.multiple_of(j * CHUNK, CHUNK), CHUNK)
        tot = of_ref[rows, :] + ob_ref[rows, :]
        y = tot * lax.rsqrt(jnp.mean(tot * tot, axis=-1, keepdims=True) + RMS_EPS) * ng_ref[...]
        o_ref[0, rows, :] = (y * _silu(g_ref[0, rows, :])).astype(BF16)
        return carry

    lax.fori_loop(0, n, readout, 0, unroll=READOUT_UNROLL if n % READOUT_UNROLL == 0 else 1)


def _hgrn(hq, bf, kf, bb, kb, hi, hgate, norm_g, consts, n_ctx):
    B, S, _ = hq.shape
    assert (S // CHUNK) % 2 == 0
    col = pl.BlockSpec((1, S, HG_DIM), lambda b, h: (b, 0, h))
    return pl.pallas_call(
        functools.partial(_hgrn_kernel, n_ctx_chunks=n_ctx // CHUNK, n_lat_chunks=(S - n_ctx) // CHUNK),
        grid=(B, HG_HEADS),
        in_specs=[col, col, col, col, col, col, col,
                  pl.BlockSpec((1, HG_DIM), lambda b, h: (0, 0))] + [_resident(a.shape) for a in consts],
        out_specs=col,
        out_shape=jax.ShapeDtypeStruct((B, S, HG_W), BF16),
        scratch_shapes=[pltpu.VMEM((S, HG_DIM), F32), pltpu.VMEM((S, HG_DIM), F32),
                        pltpu.VMEM((2, HG_DIM, HG_DIM), F32), pltpu.VMEM((2, 2, CHUNK, CHUNK), BF16),
                        pltpu.VMEM((2, 2, CHUNK, HG_DIM), BF16), pltpu.VMEM((2, 2, HG_DIM, HG_DIM), BF16)],
        compiler_params=_cparams(2),
        name="hgrn2",
    )(hq, bf, kf, bb, kb, hi, hgate, norm_g, *consts)


RING = 4
MIXER_FFN_LAG = 2
CONF_FFN_LAG = 4
CONF_ROWS = 128


def _segment_edges(i, n_tiles, n_ctx_tiles):
    is_first = (i == 0) | (i == n_ctx_tiles)
    is_last = (i == n_tiles - 1) | (i == n_ctx_tiles - 1)
    return is_first, is_last


def _ffn_tile(x, xp, xn, mod_ref, is_first, is_last, wup_ref, dww_ref, dwb_ref, wdn_ref, lng_ref, lnb_ref):
    shift = mod_ref[0, 0, 3:4, :]
    scale = mod_ref[0, 0, 4:5, :]
    gate_mod = mod_ref[0, 0, 5:6, :]
    mod = lambda a: a * (1.0 + scale) + shift
    hp = jnp.where(is_first, 0.0, mod(xp))
    hn = jnp.where(is_last, 0.0, mod(xn))
    h_ext = jnp.concatenate([hp, mod(x), hn], axis=0).astype(BF16)
    n_ext = TM + 2 * FFN_HALO
    ug = _dot(h_ext, wup_ref[:, 0:FFN_HIDDEN])
    uv = _dot(mod(x).astype(BF16), wup_ref[:, FFN_HIDDEN:])
    lo = pltpu.roll(ug, 1, axis=0)[FFN_HALO:FFN_HALO + TM]
    hi = pltpu.roll(ug, n_ext - 1, axis=0)[FFN_HALO:FFN_HALO + TM]
    conv = (lo * dww_ref[0:1, :] + ug[FFN_HALO:FFN_HALO + TM] * dww_ref[1:2, :] + hi * dww_ref[2:3, :]
            + dwb_ref[...])
    act = (_silu(conv) * uv).astype(BF16)
    y = _dot(act, wdn_ref[...])
    out = _layer_norm(DEEPNORM_ALPHA * x + gate_mod * y, lng_ref[...], lnb_ref[...])
    marks = [ug[0:SUBLANES, c:c + LANES] for c in range(0, FFN_HIDDEN, MXU_N)]
    return out, marks


def _ring_tiles(s, ring_ref, lag):
    cur = (s + 2 * RING - lag) % RING
    prv = (s + 2 * RING - lag - 1) % RING
    nxt = (s + 2 * RING - lag + 1) % RING
    return ring_ref[cur], ring_ref[prv, TM - FFN_HALO:TM, :], ring_ref[nxt, 0:FFN_HALO, :]


def _lagged_edges(s, lag, n_tiles, n_ctx_tiles, n_total):
    t = jnp.clip(s - lag, 0, n_total - 1)
    return _segment_edges(t % n_tiles, n_tiles, n_ctx_tiles)


def _conf_glu(x, xp, xn, mod_ref, is_first, is_last, w1_ref, b1_ref):
    shift = mod_ref[0, 0, 0:1, :]
    scale = mod_ref[0, 0, 1:2, :]
    n_ext = TM + 2 * CONF_HALO
    x_ext = jnp.concatenate([xp, x, xn], axis=0)
    h_ext = (x_ext * (1.0 + scale) + shift).astype(BF16)
    a = _dot(h_ext, w1_ref[...]) + b1_ref[...]
    a = a[:, :D_MODEL] * _sigmoid(a[:, D_MODEL:])
    rows = lax.broadcasted_iota(jnp.int32, (n_ext, 1), 0)
    pad = ((rows < CONF_HALO) & is_first) | ((rows >= CONF_HALO + TM) & is_last)
    return jnp.where(pad, 0.0, a)


def _conf_dwconv(a_ref, ia, c_ref, ic, dww_ref, dwb_ref, marks, zero_ref):
    half = (CONF_KERNEL - 1) // 2
    base = CONF_HALO - half
    n_slab = CONF_ROWS + 2 * CONF_HALO

    for cb in range(D_MODEL // LANES):
        lanes = slice(cb * LANES, (cb + 1) * LANES)
        for rb in range(TM // CONF_ROWS):
            slab = a_ref[ia, rb * CONF_ROWS:rb * CONF_ROWS + n_slab, lanes]
            mark = marks[(cb * (TM // CONF_ROWS) + rb) % len(marks)]
            edge = pltpu.bitcast(pltpu.bitcast(mark, jnp.int32) & zero_ref[...], F32)
            acc = jnp.broadcast_to(dwb_ref[:, lanes] + edge[0:1, :], (CONF_ROWS, LANES))
            for r in range(SUBLANES):
                sh = pltpu.roll(slab, n_slab - r, axis=0) if r else slab
                for mm in range(-(-(base + CONF_KERNEL) // SUBLANES)):
                    kk = SUBLANES * mm + r - base
                    if 0 <= kk < CONF_KERNEL:
                        acc = acc + sh[SUBLANES * mm:SUBLANES * mm + CONF_ROWS] * dww_ref[kk:kk + 1, lanes]
            c_ref[ic, rb * CONF_ROWS:(rb + 1) * CONF_ROWS, lanes] = acc


def _conf_out(conv, x, mod_ref, cg_ref, cb_ref, w2_ref, b2_ref, lng_ref, lnb_ref):
    gate_mod = mod_ref[0, 0, 2:3, :]
    z = _silu(_layer_norm(conv, cg_ref[...], cb_ref[...])).astype(BF16)
    y = _dot(z, w2_ref[...]) + b2_ref[...]
    return _layer_norm(DEEPNORM_ALPHA * x + gate_mod * y, lng_ref[...], lnb_ref[...])


def _zero_once(s, *refs):
    @pl.when(s == 0)
    def _():
        for ref in refs:
            ref[...] = jnp.zeros(ref.shape, F32)


def _conf_ffn_kernel(x_ref, xp_ref, xn_ref, xres_ref, moda_ref, modc_ref, zero_ref, modf_ref,
                     w1_ref, b1_ref, cdww_ref, cdwb_ref, cg_ref, cb_ref, w2_ref, b2_ref, tlng_ref, tlnb_ref,
                     wup_ref, fdww_ref, fdwb_ref, wdn_ref, flng_ref, flnb_ref,
                     o_ref, ring_ref, a_ref, c_ref, *, n_tiles, n_ctx_tiles, n_total):
    s = pl.program_id(0)
    _zero_once(s, ring_ref, a_ref, c_ref)
    edges = functools.partial(_lagged_edges, s, n_tiles=n_tiles, n_ctx_tiles=n_ctx_tiles, n_total=n_total)
    ffn_in = _ring_tiles(s, ring_ref, CONF_FFN_LAG)
    ring_ref[(s + 2) % RING] = _conf_out(c_ref[s % 2], xres_ref[0], modc_ref, cg_ref, cb_ref, w2_ref, b2_ref,
                                         tlng_ref, tlnb_ref)
    o_ref[0], marks = _ffn_tile(*ffn_in, modf_ref, *edges(CONF_FFN_LAG),
                                wup_ref, fdww_ref, fdwb_ref, wdn_ref, flng_ref, flnb_ref)
    _conf_dwconv(a_ref, (s + 1) % 2, c_ref, (s + 1) % 2, cdww_ref, cdwb_ref, marks, zero_ref)
    a_ref[s % 2] = _conf_glu(x_ref[0], xp_ref[0], xn_ref[0], moda_ref, *edges(0), w1_ref, b1_ref)


def _mixout_ffn_kernel(att_ref, hg_ref, *refs, n_streams, n_tiles, n_ctx_tiles, n_total):
    x_refs, refs = refs[:n_streams], refs[n_streams:]
    (modt_ref, modf_ref, wo_ref, tlng_ref, tlnb_ref, wup_ref, fdww_ref, fdwb_ref, wdn_ref, flng_ref, flnb_ref,
     o_ref, ring_ref) = refs
    s = pl.program_id(0)
    _zero_once(s, ring_ref)
    ffn_in = _ring_tiles(s, ring_ref, MIXER_FFN_LAG)
    y = _dot(att_ref[0], wo_ref[0:ATT_Q_W, :]) + _dot(hg_ref[0], wo_ref[ATT_Q_W:, :])
    o_ref[0], _ = _ffn_tile(*ffn_in, modf_ref, *_lagged_edges(s, MIXER_FFN_LAG, n_tiles, n_ctx_tiles, n_total),
                            wup_ref, fdww_ref, fdwb_ref, wdn_ref, flng_ref, flnb_ref)
    gate = modt_ref[0, 0, 2:3, :]
    x = _stream_tile(x_refs, jnp.minimum(s, n_total - 1) % n_tiles, n_ctx_tiles)
    ring_ref[s % RING] = _layer_norm(DEEPNORM_ALPHA * x + gate * y, tlng_ref[...], tlnb_ref[...])


def _tail_ffn(kind, acts, mods, tail_params, ffn_params, n_in_ctx_tiles, skip_ctx):
    B, S, _ = acts[0].shape
    off = n_in_ctx_tiles if skip_ctx else 0
    nt = S // TM - off
    nct = 0 if skip_ctx else n_in_ctx_tiles
    n_total = B * nt
    ffn_lag = CONF_FFN_LAG if kind == "conformer" else MIXER_FFN_LAG

    def lagged(lag):
        def tile(s):
            t = jnp.clip(s - lag, 0, n_total - 1)
            return t // nt, t % nt
        return tile

    tail_tile = lagged(0)

    def rows(w, tile_fn=tail_tile):
        def index(s):
            b, i = tile_fn(s)
            return (b, i + off, 0)
        return pl.BlockSpec((1, TM, w), index)

    def halo(before):
        per = TM // CONF_HALO

        def index(s):
            b, i = tail_tile(s)
            blk = (i + off) * per - 1 if before else (i + off + 1) * per
            return (b, jnp.clip(blk, 0, (S // TM) * per - 1), 0)
        return pl.BlockSpec((1, CONF_HALO, D_MODEL), index)

    def mod_spec(tile_fn):
        def index(s):
            b, i = tile_fn(s)
            return (b, jnp.where(i < nct, 0, 1), 0, 0)
        return pl.BlockSpec((1, 1, 6, D_MODEL), index)

    def out_index(s):
        b, i = lagged(ffn_lag)(s)
        return (b, i, 0)

    small = lambda a: pl.BlockSpec(a.shape, lambda s: (0,) * a.ndim)
    spec_of = lambda a: _resident(a.shape) if a.size * a.dtype.itemsize > (1 << 20) else small(a)
    scratch = [pltpu.VMEM((RING, TM, D_MODEL), F32)]
    if kind == "conformer":
        xs, = acts
        body = _conf_ffn_kernel
        zero = jnp.zeros((SUBLANES, LANES), jnp.int32)
        in_specs = [rows(D_MODEL), halo(True), halo(False), rows(D_MODEL, lagged(2)),
                    mod_spec(tail_tile), mod_spec(lagged(2)), small(zero)]
        operands = [xs, xs, xs, xs, mods, mods, zero]
        scratch += [pltpu.VMEM((2, TM + 2 * CONF_HALO, D_MODEL), F32), pltpu.VMEM((2, TM, D_MODEL), F32)]
    else:
        streams = acts[2:]
        assert len(streams) == 1 or not skip_ctx
        body = functools.partial(_mixout_ffn_kernel, n_streams=len(streams))
        in_specs = [rows(a.shape[2]) for a in acts[:2]]
        in_specs += [rows(D_MODEL)] if len(streams) == 1 else _stream_specs(streams, nct, tail_tile)
        in_specs += [mod_spec(tail_tile)]
        operands = list(acts) + [mods]
    in_specs += [mod_spec(lagged(ffn_lag))] + [spec_of(a) for a in tail_params + ffn_params]
    operands += [mods] + list(tail_params) + list(ffn_params)
    return pl.pallas_call(
        functools.partial(body, n_tiles=nt, n_ctx_tiles=nct, n_total=n_total),
        grid=(n_total + ffn_lag,),
        in_specs=in_specs,
        out_specs=pl.BlockSpec((1, TM, D_MODEL), out_index),
        out_shape=jax.ShapeDtypeStruct((B, nt * TM, D_MODEL), F32),
        scratch_shapes=scratch,
        compiler_params=_cparams(1),
        name=kind + "_ffn",
    )(*operands)


def _rope_tables(n_ctx, n_lat):
    rows = n_lat // GRID_W
    row = jnp.repeat(jnp.arange(rows), GRID_W).astype(F32)
    col = jnp.tile(jnp.arange(GRID_W), rows).astype(F32)
    quarter = ATT_HEAD_DIM // 4
    inv_freq = ROPE_BASE ** (-jnp.arange(quarter, dtype=F32) / quarter)
    ang_r = row[:, None] * inv_freq
    ang_c = col[:, None] * inv_freq
    ang = jnp.concatenate([ang_r, ang_r, ang_c, ang_c], axis=-1)
    ang = jnp.concatenate([jnp.zeros((n_ctx, ATT_HEAD_DIM), F32), ang], axis=0)
    ang = jnp.concatenate([ang, ang], axis=-1)
    cos, sin = jnp.cos(ang), jnp.sin(ang)
    low = (jnp.arange(LANES) % (2 * quarter)) < quarter
    return cos, jnp.where(low, -sin, 0.0), jnp.where(low, 0.0, sin)


def _chunk_constants():
    t = jnp.arange(CHUNK)
    full = lambda col: jnp.broadcast_to(col[:, None], (CHUNK, HG_DIM)).astype(F32)
    sgn, mask = [], []
    for d in range(2):
        sgn_d, mask_d = [], []
        for lv in range(CHUNK_LEVELS):
            is_q = ((t >> lv) & 1) == (1 - d)
            same = (t[:, None] >> (lv + 1)) == (t[None, :] >> (lv + 1))
            sgn_d.append(full(jnp.where(is_q, 1.0, -1.0)))
            mask_d.append((same & is_q[:, None] & ~is_q[None, :]).astype(F32))
        mask_d.append((t[:, None] == t[None, :]).astype(F32))
        sgn.append(jnp.stack(sgn_d))
        mask.append(jnp.stack(mask_d))
    odd = (t & 1) == 1
    r = t & 3
    coef = jnp.stack([
        jnp.stack([full(odd), full(~odd), full(r == 3), full(r == 2), full(r == 1), full(r == 0)]),
        jnp.stack([full(~odd), full(odd), full(r == 0), full(r == 3), full(r == 2), full(r == 1)]),
    ])
    return jnp.stack(sgn), jnp.stack(mask), coef


def _prefix_matrices():
    t = jnp.arange(CHUNK)
    lower = t[None, :] <= t[:, None]
    return jnp.stack([lower, lower.T]).astype(BF16)


def kernel(x, c, ctx, c_ctx, mod_w, mod_b, post_ln_g, post_ln_b, mix_w_in, mix_w_out, att_sink, hg_lb_logits, hg_norm_g, conf_pw1_w, conf_pw1_b, conf_dw_w, conf_dw_b, conf_ln_g, conf_ln_b, conf_pw2_w, conf_pw2_b, ffn_w_up, ffn_dw_w, ffn_dw_b, ffn_w_down):
    B, T, D = x.shape
    L = ctx.shape[1]
    assert D == D_MODEL and L % TM == 0 and T % TM == 0 and L >= ATT_BLOCK and T % GRID_W == 0
    n_ctx_tiles = L // TM

    n_cond = -(-(B + 1) // SUBLANES) * SUBLANES
    cond = jnp.zeros((n_cond, D), F32).at[:B].set(c).at[B].set(c_ctx)
    mod = _modulation(cond, mod_w.astype(BF16), mod_b[:, None, :])
    mod_x = mod[:, :B].reshape(DEPTH, B, 1, 6, D)
    mod_c = jnp.broadcast_to(mod[:, B].reshape(DEPTH, 1, 1, 6, D), (DEPTH, B, 1, 6, D))
    mods = jnp.concatenate([mod_c, mod_x], axis=2)

    cos, sa, sb = _rope_tables(L, T)
    chunk_consts = _chunk_constants()
    tri = _prefix_matrices()
    vec = lambda a: a.reshape(1, -1)

    streams = (ctx, x)
    has_ctx = True
    for layer in range(DEPTH):
        m = layer // 2
        need_ctx_out = any(j % 2 == 0 for j in range(layer + 1, DEPTH))
        nct = n_ctx_tiles if has_ctx else 0
        lg, lb = post_ln_g[layer], post_ln_b[layer]
        ffn_params = (ffn_w_up[layer].astype(BF16), ffn_dw_w[layer], vec(ffn_dw_b[layer]),
                      ffn_w_down[layer].astype(BF16), vec(lg[1]), vec(lb[1]))
        if layer % 2 == 0:
            q, k, v, hq, bf, kf, bb, kb, hi, hgate = _mix_in(
                streams, mods[layer], mix_w_in[m].astype(BF16), cos, sa, sb, hg_lb_logits, tri, m, nct)
            att = _attention(att_sink[m], q, k, v, L, need_ctx_out)
            hg = _hgrn(hq, bf, kf, bb, kb, hi, hgate, vec(hg_norm_g[m]), chunk_consts, L)
            tail_params = (mix_w_out[m].astype(BF16), vec(lg[0]), vec(lb[0]))
            xs = _tail_ffn("mixer", (att, hg) + streams, mods[layer], tail_params, ffn_params, nct,
                           not need_ctx_out)
        else:
            tail_params = (conf_pw1_w[m].astype(BF16), vec(conf_pw1_b[m]), conf_dw_w[m], vec(conf_dw_b[m]),
                           vec(conf_ln_g[m]), vec(conf_ln_b[m]), conf_pw2_w[m].astype(BF16), vec(conf_pw2_b[m]),
                           vec(lg[0]), vec(lb[0]))
            xs = _tail_ffn("conformer", streams, mods[layer], tail_params, ffn_params, nct,
                           has_ctx and not need_ctx_out)
        streams = (xs,)
        has_ctx = has_ctx and need_ctx_out
    return xs[:, L:] if has_ctx else xs
```

```python
import functools

import jax
import jax.numpy as jnp
from jax import lax
from jax.experimental import pallas as pl
from jax.experimental.pallas import tpu as pltpu

F32 = jnp.float32
BF16 = jnp.bfloat16

D_MODEL = 1024
DEPTH = 4
GRID_W = 64
ATT_HEADS = 8
ATT_KV_HEADS = 2
ATT_HEAD_DIM = 64
ATT_WINDOW = 128
ATT_BLOCK = 128
ROPE_BASE = 10000.0
HG_HEADS = 4
HG_DIM = 128
CONF_KERNEL = 31
FFN_HIDDEN = 2816
DEEPNORM_ALPHA = (2 * DEPTH) ** 0.25
LN_EPS = 1e-5
RMS_EPS = 1e-6
MASK_VALUE = -1e30
LB_FLOOR = 1e-30

ATT_Q_W = ATT_HEADS * ATT_HEAD_DIM
ATT_KV_W = ATT_KV_HEADS * ATT_HEAD_DIM
HG_W = HG_HEADS * HG_DIM
OFF_AQ = 0
OFF_AK = OFF_AQ + ATT_Q_W
OFF_AV = OFF_AK + ATT_KV_W
OFF_HQ = OFF_AV + ATT_KV_W
OFF_FF = OFF_HQ + HG_W
OFF_FB = OFF_FF + HG_W
OFF_HI = OFF_FB + HG_W
OFF_HGATE = OFF_HI + HG_W
MIX_IN_W = OFF_HGATE + HG_W

LANES = 128
SUBLANES = 8
MXU_N = 256
TM = 256
CHUNK = 128
CHUNK_LEVELS = 7
READOUT_UNROLL = 6
FFN_HALO = SUBLANES
CONF_HALO = 16
VMEM_LIMIT = 56 * 1024 * 1024


def _cparams(n_grid):
    return pltpu.CompilerParams(
        dimension_semantics=("arbitrary",) * n_grid, vmem_limit_bytes=VMEM_LIMIT)


def _dot(a, b):
    return jnp.dot(a, b, preferred_element_type=F32)


def _dot_nt(a, b):
    return lax.dot_general(a, b, (((1,), (1,)), ((), ())), preferred_element_type=F32)


def _dot_tn(a, b):
    return lax.dot_general(a, b, (((0,), (0,)), ((), ())), preferred_element_type=F32)


def _sigmoid(x):
    return 0.5 * jnp.tanh(0.5 * x) + 0.5


def _silu(x):
    h = 0.5 * x
    return h * (1.0 + jnp.tanh(h))


def _layer_norm(z, g, b):
    mu = jnp.mean(z, axis=-1, keepdims=True)
    zc = z - mu
    var = jnp.mean(zc * zc, axis=-1, keepdims=True)
    return zc * lax.rsqrt(var + LN_EPS) * g + b


def _resident(shape):
    nd = len(shape)
    return pl.BlockSpec(shape, lambda *_: (0,) * nd, pipeline_mode=pl.Buffered(1))


MOD_TN = 1536


def _mod_kernel(cond_ref, w_ref, b_ref, o_ref):
    s = _silu(cond_ref[...])
    hi = s.astype(BF16)
    lo = (s - hi.astype(F32)).astype(BF16)
    w = w_ref[0]
    o_ref[0] = _dot(hi, w) + _dot(lo, w) + b_ref[0]


def _modulation(cond, mod_w, mod_b):
    R = cond.shape[0]
    n6 = mod_w.shape[2]
    return pl.pallas_call(
        _mod_kernel,
        grid=(DEPTH, n6 // MOD_TN),
        in_specs=[
            pl.BlockSpec((R, D_MODEL), lambda l, j: (0, 0)),
            pl.BlockSpec((1, D_MODEL, MOD_TN), lambda l, j: (l, 0, j)),
            pl.BlockSpec((1, 1, MOD_TN), lambda l, j: (l, 0, j)),
        ],
        out_specs=pl.BlockSpec((1, R, MOD_TN), lambda l, j: (l, 0, j)),
        out_shape=jax.ShapeDtypeStruct((DEPTH, R, n6), F32),
        compiler_params=_cparams(2),
        name="modulation",
    )(cond, mod_w, mod_b)


def _stream_tile(x_refs, i, n_ctx_tiles):
    if len(x_refs) == 1:
        return x_refs[0][0]
    return jnp.where(i < n_ctx_tiles, x_refs[0][0], x_refs[1][0])


def _stream_specs(streams, n_ctx_tiles, tile_fn):
    if len(streams) == 1:
        def index(*g):
            b, i = tile_fn(*g)
            return (b, i, 0)
        return [pl.BlockSpec((1, TM, D_MODEL), index)]

    def ctx_index(*g):
        b, i = tile_fn(*g)
        return (b, jnp.minimum(i, n_ctx_tiles - 1), 0)

    def lat_index(*g):
        b, i = tile_fn(*g)
        return (b, jnp.maximum(i - n_ctx_tiles, 0), 0)
    return [pl.BlockSpec((1, TM, D_MODEL), ctx_index), pl.BlockSpec((1, TM, D_MODEL), lat_index)]


def _mix_in_kernel(*refs, layer_m, n_streams, n_ctx_tiles):
    x_refs, refs = refs[:n_streams], refs[n_streams:]
    (mod_ref, w_ref, cos_ref, sa_ref, sb_ref, lbl_ref, tri_ref,
     q_ref, k_ref, v_ref, hq_ref, bf_ref, kf_ref, bb_ref, kb_ref, hi_ref, hg_ref) = refs
    x = _stream_tile(x_refs, pl.program_id(1), n_ctx_tiles)
    shift = mod_ref[0, 0, 0:1, :]
    scale = mod_ref[0, 0, 1:2, :]
    h = (x * (1.0 + scale) + shift).astype(BF16)

    def proj(off, width):
        return _dot(h, w_ref[:, off:off + width])

    fr_fwd = proj(OFF_FF, HG_W)
    fr_bwd = proj(OFF_FB, HG_W)

    cos = cos_ref[...]
    sa = sa_ref[...]
    sb = sb_ref[...]

    def rope(a, reps):
        w = a.shape[1]
        c = jnp.concatenate([cos] * reps, axis=1) if reps > 1 else cos
        s1 = jnp.concatenate([sa] * reps, axis=1) if reps > 1 else sa
        s2 = jnp.concatenate([sb] * reps, axis=1) if reps > 1 else sb
        up = pltpu.roll(a, w - 16, axis=1)
        dn = pltpu.roll(a, 16, axis=1)
        return a * c + up * s1 + dn * s2

    q = rope(proj(OFF_AQ, ATT_Q_W), ATT_Q_W // LANES) * (ATT_HEAD_DIM ** -0.5)
    q_ref[0] = q.astype(BF16)

    lane = lax.broadcasted_iota(jnp.int32, (TM, LANES), 1)
    first = lane < ATT_HEAD_DIM

    def pair_rep(a):
        sw = pltpu.roll(a, ATT_HEAD_DIM, axis=1)
        return jnp.concatenate([jnp.where(first, a, sw), jnp.where(first, sw, a)], axis=1)

    k_ref[0] = pair_rep(rope(proj(OFF_AK, ATT_KV_W), 1)).astype(BF16)
    v_ref[0] = pair_rep(proj(OFF_AV, ATT_KV_W)).astype(BF16)

    logits = lbl_ref[...]
    n_mix = logits.shape[0]
    mx = logits[0:1, :]
    for r in range(1, n_mix):
        mx = jnp.maximum(mx, logits[r:r + 1, :])
    ex = [jnp.exp(logits[r:r + 1, :] - mx) for r in range(n_mix)]
    tot = ex[0]
    for r in range(1, n_mix):
        tot = tot + ex[r]
    cum = ex[0] / tot
    p0 = cum
    for r in range(1, layer_m + 1):
        cum = cum + ex[r] / tot
    lb = cum - p0
    lb_floor = jnp.maximum(lb, LB_FLOOR)
    one_m = 1.0 - lb

    def gates(fr, d, b_out, k_out):
        e = jnp.exp(-jnp.abs(fr))
        r = 1.0 / (1.0 + e)
        er = e * r
        pos = fr >= 0
        k_out[0] = one_m * jnp.where(pos, er, r)
        lf = jnp.log2(lb_floor + one_m * jnp.where(pos, r, er))
        tri = tri_ref[d]
        for c in range(TM // CHUNK):
            g = lf[c * CHUNK:(c + 1) * CHUNK]
            g_hi = g.astype(BF16)
            g_lo = (g - g_hi.astype(F32)).astype(BF16)
            b_out[0, c * CHUNK:(c + 1) * CHUNK, :] = _dot(tri, g_hi) + _dot(tri, g_lo)

    hq_ref[0] = proj(OFF_HQ, HG_W)
    gates(fr_fwd, 0, bf_ref, kf_ref)
    hi_ref[0] = proj(OFF_HI, HG_W).astype(BF16)
    gates(fr_bwd, 1, bb_ref, kb_ref)
    hg_ref[0] = proj(OFF_HGATE, HG_W)


def _mix_in(streams, mods, w_in, cos, sa, sb, lb_logits, tri, layer_m, n_ctx_tiles):
    B = streams[0].shape[0]
    S = sum(a.shape[1] for a in streams)
    nt = S // TM
    row = lambda w: pl.BlockSpec((1, TM, w), lambda b, i: (b, i, 0))
    tab = pl.BlockSpec((TM, LANES), lambda b, i: (i, 0))
    sds = lambda w, dt: jax.ShapeDtypeStruct((B, S, w), dt)
    return pl.pallas_call(
        functools.partial(_mix_in_kernel, layer_m=layer_m, n_streams=len(streams), n_ctx_tiles=n_ctx_tiles),
        grid=(B, nt),
        in_specs=_stream_specs(streams, n_ctx_tiles, lambda b, i: (b, i)) + [
            pl.BlockSpec((1, 1, 6, D_MODEL), lambda b, i: (b, jnp.where(i < n_ctx_tiles, 0, 1), 0, 0)),
            _resident((D_MODEL, MIX_IN_W)),
            tab, tab, tab,
            _resident(lb_logits.shape),
            _resident(tri.shape),
        ],
        out_specs=[row(ATT_Q_W), row(2 * LANES), row(2 * LANES), row(HG_W), row(HG_W), row(HG_W),
                   row(HG_W), row(HG_W), row(HG_W), row(HG_W)],
        out_shape=[sds(ATT_Q_W, BF16), sds(2 * LANES, BF16), sds(2 * LANES, BF16), sds(HG_W, F32),
                   sds(HG_W, F32), sds(HG_W, F32), sds(HG_W, F32), sds(HG_W, F32), sds(HG_W, BF16),
                   sds(HG_W, F32)],
        compiler_params=_cparams(2),
        name="mix_in",
    )(*streams, mods, w_in, cos, sa, sb, lb_logits, tri)


GROUP = ATT_HEADS // ATT_KV_HEADS


def _attn_kernel(sink_ref, q_ref, k_ref, v_ref, o_ref, p_ref, rden_ref, *, n_ctx, n_lat, ctx_out):
    hkv = pl.program_id(1)
    QB = ATT_BLOCK
    nb = n_lat // QB
    lane = lax.broadcasted_iota(jnp.int32, (QB, LANES), 1)
    first = lane < ATT_HEAD_DIM
    rows4 = lax.broadcasted_iota(jnp.int32, (GROUP * QB, LANES), 0)
    rq = rows4 & (QB - 1)
    col = lax.broadcasted_iota(jnp.int32, (GROUP * QB, LANES), 1)
    grp = lax.broadcasted_iota(jnp.int32, (GROUP * QB, 1), 0) // QB
    sink = jnp.zeros((GROUP * QB, 1), F32)
    for g in range(GROUP):
        sink = jnp.where(grp == g, sink_ref[hkv * GROUP + g], sink)

    def stack_q(q):
        qa, qb = q[:, :LANES], q[:, LANES:]
        z = jnp.zeros_like(qa)
        return jnp.concatenate([jnp.where(first, qa, z), jnp.where(first, z, qa),
                                jnp.where(first, qb, z), jnp.where(first, z, qb)], axis=0)

    def unstack_o(o):
        return jnp.concatenate([jnp.where(first, o[0:QB], o[QB:2 * QB]),
                                jnp.where(first, o[2 * QB:3 * QB], o[3 * QB:4 * QB])], axis=1)

    kc = k_ref[0, 0:n_ctx, :]
    vc = v_ref[0, 0:n_ctx, :]
    tiles = lambda a: [a[:, c:c + LANES] for c in range(0, a.shape[1], LANES)]

    def softmax_parts(parts):
        mt = None
        for s in parts:
            for t in tiles(s):
                mt = t if mt is None else jnp.maximum(mt, t)
        m = jnp.maximum(sink, jnp.max(mt, axis=-1, keepdims=True))
        dt = None
        ps = []
        for s in parts:
            p = jnp.exp(s - m)
            for t in tiles(p):
                dt = t if dt is None else dt + t
            ps.append(p.astype(BF16))
        den = jnp.exp(sink - m) + jnp.sum(dt, axis=-1, keepdims=True)
        return ps, 1.0 / den

    def block_rows(i):
        r0 = pl.multiple_of(n_ctx + i * QB, QB)
        rp = pl.multiple_of(r0 - QB, QB)
        rn = pl.multiple_of(jnp.minimum(r0 + QB, n_ctx + n_lat - QB), QB)
        return rp, r0, rn

    def qk_scores(i):
        rp, r0, rn = block_rows(i)
        q4 = stack_q(q_ref[0, pl.ds(r0, QB), :])
        lo_col = jnp.where(i > 0, rq, LANES)
        hi_col = jnp.where(i < nb - 1, rq, -1)
        sp = jnp.where(col >= lo_col, _dot_nt(q4, k_ref[0, pl.ds(rp, QB), :]), MASK_VALUE)
        ss = _dot_nt(q4, k_ref[0, pl.ds(r0, QB), :])
        sn = jnp.where(col <= hi_col, _dot_nt(q4, k_ref[0, pl.ds(rn, QB), :]), MASK_VALUE)
        return [sp, ss, sn, _dot_nt(q4, kc)]

    def store_probs(parts, slot):
        ps, rden = softmax_parts(parts)
        p_ref[slot] = jnp.concatenate(ps, axis=1)
        rden_ref[slot] = jnp.broadcast_to(rden, (GROUP * QB, LANES))

    def output_stage(i, slot):
        rp, r0, rn = block_rows(i)
        acc = (_dot(p_ref[slot, :, 0:QB], v_ref[0, pl.ds(rp, QB), :])
               + _dot(p_ref[slot, :, QB:2 * QB], v_ref[0, pl.ds(r0, QB), :])
               + _dot(p_ref[slot, :, 2 * QB:3 * QB], v_ref[0, pl.ds(rn, QB), :])
               + _dot(p_ref[slot, :, 3 * QB:], vc))
        o_ref[0, pl.ds(r0, QB), :] = unstack_o(acc * rden_ref[slot]).astype(BF16)

    p_ref[...] = jnp.zeros(p_ref.shape, BF16)
    rden_ref[...] = jnp.zeros(rden_ref.shape, F32)

    def trip(t, carry):
        i0 = 2 * t
        s_a = qk_scores(i0)
        output_stage(jnp.maximum(i0 - 1, 0), 1)
        s_b = qk_scores(i0 + 1)
        store_probs(s_a, 0)
        output_stage(i0, 0)
        store_probs(s_b, 1)
        return carry

    lax.fori_loop(0, nb // 2, trip, 0, unroll=True)
    output_stage(nb - 1, 1)

    for j in range(n_ctx // QB):
        if ctx_out:
            q4 = stack_q(q_ref[0, j * QB:(j + 1) * QB, :])
            (p,), rden = softmax_parts([_dot_nt(q4, kc)])
            o_ref[0, j * QB:(j + 1) * QB, :] = unstack_o(_dot(p, vc) * rden).astype(BF16)
        else:
            o_ref[0, j * QB:(j + 1) * QB, :] = jnp.zeros((QB, 2 * LANES), BF16)


def _attention(sink, q, k, v, n_ctx, ctx_out):
    B, S, _ = q.shape
    return pl.pallas_call(
        functools.partial(_attn_kernel, n_ctx=n_ctx, n_lat=S - n_ctx, ctx_out=ctx_out),
        grid=(B, ATT_KV_HEADS),
        in_specs=[
            pl.BlockSpec(memory_space=pltpu.SMEM),
            pl.BlockSpec((1, S, 2 * LANES), lambda b, h: (b, 0, h)),
            pl.BlockSpec((1, S, LANES), lambda b, h: (b, 0, h)),
            pl.BlockSpec((1, S, LANES), lambda b, h: (b, 0, h)),
        ],
        out_specs=pl.BlockSpec((1, S, 2 * LANES), lambda b, h: (b, 0, h)),
        out_shape=jax.ShapeDtypeStruct((B, S, ATT_Q_W), BF16),
        scratch_shapes=[pltpu.VMEM((2, GROUP * ATT_BLOCK, 3 * ATT_BLOCK + n_ctx), BF16),
                        pltpu.VMEM((2, GROUP * ATT_BLOCK, LANES), F32)],
        compiler_params=_cparams(2),
        name="attention",
    )(sink, q, k, v)


def _hgrn_scores(q, b, k, v, d, slot, st_ref, sgn_ref, mask_ref, coef_ref, sc_ref, qh_ref, sti_ref):
    C = CHUNK
    SUB = SUBLANES
    rev = d == 1
    b_p1 = pltpu.roll(b, 1, axis=0)
    b_p2 = pltpu.roll(b, 2, axis=0)
    b_n1 = pltpu.roll(b, C - 1, axis=0)
    b_n2 = pltpu.roll(b, C - 2, axis=0)
    rows_of = lambda a, i, n: a[i * n:(i + 1) * n]
    sc = [None] * (C // SUB)

    def add_rows(first_row, p):
        for i in range(p.shape[0] // SUB):
            j = first_row // SUB + i
            blk = p[i * SUB:(i + 1) * SUB]
            sc[j] = blk if sc[j] is None else sc[j] + blk

    for lv in range(CHUNK_LEVELS):
        half = 1 << lv
        if half < SUB:
            sgn = sgn_ref[d, lv]
            if lv == 0:
                bm = (b_n1 if rev else b_p1) * coef_ref[d, 0] + b * coef_ref[d, 1]
            elif lv == 1:
                far = b_n2 if rev else b_p2
                bm = far * coef_ref[d, 2] + b_p1 * coef_ref[d, 3] + b * coef_ref[d, 4] + b_n1 * coef_ref[d, 5]
            else:
                nblk = C // (2 * half)
                b3 = b.reshape(nblk, 2 * half, HG_DIM)
                ref_row = half if rev else half - 1
                bm = jnp.broadcast_to(b3[:, ref_row:ref_row + 1, :], (nblk, 2 * half, HG_DIM)).reshape(C, HG_DIM)
            x = (jnp.where(sgn > 0, q, k) * jnp.exp2((b - bm) * sgn)).astype(BF16)
            add_rows(0, _dot_nt(x, x) * mask_ref[d, lv])
        else:
            xs, xq, q_first = [], [], []
            for blk in range(C // half):
                is_q = (blk % 2 == 0) if rev else (blk % 2 == 1)
                pair0 = (blk // 2) * 2 * half
                ref = pair0 + (half if rev else half - 1)
                bb = rows_of(b, blk, half)
                bm = b[ref:ref + 1, :]
                xb = ((rows_of(q, blk, half) * jnp.exp2(bb - bm)) if is_q
                      else (rows_of(k, blk, half) * jnp.exp2(bm - bb))).astype(BF16)
                xs.append(xb)
                if is_q:
                    xq.append(xb)
                    q_first.append(blk * half)
            p = _dot_nt(jnp.concatenate(xq, axis=0), jnp.concatenate(xs, axis=0))
            for i, r0 in enumerate(q_first):
                add_rows(r0, p[i * half:(i + 1) * half] * mask_ref[d, lv, r0:r0 + half, :])
    diag = jnp.sum(q * k, axis=-1, keepdims=True)
    scores = jnp.concatenate(sc, axis=0) + diag * mask_ref[d, CHUNK_LEVELS]
    b_end = b[0:1, :] if rev else b[C - 1:C, :]
    st = st_ref[d]
    sc_ref[d, slot] = scores.astype(BF16)
    qh_ref[d, slot] = (q * jnp.exp2(b)).astype(BF16)
    sti_ref[d, slot] = st.astype(BF16)
    kh = (k * jnp.exp2(b_end - b)).astype(BF16)
    st_ref[d] = st * jnp.exp2(b_end) + _dot_tn(v, kh)


def _hgrn_output(v, d, slot, sc_ref, qh_ref, sti_ref):
    return _dot(sc_ref[d, slot], v) + _dot_nt(qh_ref[d, slot], sti_ref[d, slot])


def _hgrn_kernel(q_ref, bf_ref, kf_ref, bb_ref, kb_ref, v_ref, g_ref, ng_ref, sgn_ref, mask_ref,
                 coef_ref, o_ref, of_ref, ob_ref, st_ref, sc_ref, qh_ref, sti_ref, *, n_ctx_chunks, n_lat_chunks):
    nc, nl = n_ctx_chunks, n_lat_chunks
    n = nc + nl
    st_ref[...] = jnp.zeros(st_ref.shape, F32)
    sc_ref[...] = jnp.zeros(sc_ref.shape, BF16)
    qh_ref[...] = jnp.zeros(qh_ref.shape, BF16)
    sti_ref[...] = jnp.zeros(sti_ref.shape, BF16)
    b_refs = (bf_ref, bb_ref)
    k_refs = (kf_ref, kb_ref)
    o_refs = (of_ref, ob_ref)
    consts = (st_ref, sgn_ref, mask_ref, coef_ref, sc_ref, qh_ref, sti_ref)

    def rows_of_step(j, d):
        c = j if d == 0 else jnp.where(j < nc, nc - 1 - j, 2 * nc + nl - 1 - j)
        return pl.ds(pl.multiple_of(c * CHUNK, CHUNK), CHUNK)

    def first_half(j, slot):
        for d in range(2):
            r = rows_of_step(j, d)
            _hgrn_scores(q_ref[0, r, :], b_refs[d][0, r, :], k_refs[d][0, r, :], v_ref[0, r, :], d, slot, *consts)

    def second_half(j, slot):
        for d in range(2):
            r = rows_of_step(j, d)
            o_refs[d][r, :] = _hgrn_output(v_ref[0, r, :], d, slot, sc_ref, qh_ref, sti_ref)

    def trip(i, carry):
        j0 = 2 * i
        first_half(j0, 0)
        second_half(jnp.maximum(j0 - 1, 0), 1)
        first_half(j0 + 1, 1)
        second_half(j0, 0)
        return carry

    lax.fori_loop(0, n // 2, trip, 0, unroll=True)
    second_half(n - 1, 1)

    def readout(j, carry):
        rows = pl.ds(pl.multiple_of(j * CHUNK, CHUNK), CHUNK)
        tot = of_ref[rows, :] + ob_ref[rows, :]
        y = tot * lax.rsqrt(jnp.mean(tot * tot, axis=-1, keepdims=True) + RMS_EPS) * ng_ref[...]
        o_ref[0, rows, :] = (y * _silu(g_ref[0, rows, :])).astype(BF16)
        return carry

    lax.fori_loop(0, n, readout, 0, unroll=READOUT_UNROLL if n % READOUT_UNROLL == 0 else 1)


def _hgrn(hq, bf, kf, bb, kb, hi, hgate, norm_g, consts, n_ctx):
    B, S, _ = hq.shape
    assert (S // CHUNK) % 2 == 0
    col = pl.BlockSpec((1, S, HG_DIM), lambda b, h: (b, 0, h))
    return pl.pallas_call(
        functools.partial(_hgrn_kernel, n_ctx_chunks=n_ctx // CHUNK, n_lat_chunks=(S - n_ctx) // CHUNK),
        grid=(B, HG_HEADS),
        in_specs=[col, col, col, col, col, col, col,
                  pl.BlockSpec((1, HG_DIM), lambda b, h: (0, 0))] + [_resident(a.shape) for a in consts],
        out_specs=col,
        out_shape=jax.ShapeDtypeStruct((B, S, HG_W), BF16),
        scratch_shapes=[pltpu.VMEM((S, HG_DIM), F32), pltpu.VMEM((S, HG_DIM), F32),
                        pltpu.VMEM((2, HG_DIM, HG_DIM), F32), pltpu.VMEM((2, 2, CHUNK, CHUNK), BF16),
                        pltpu.VMEM((2, 2, CHUNK, HG_DIM), BF16), pltpu.VMEM((2, 2, HG_DIM, HG_DIM), BF16)],
        compiler_params=_cparams(2),
        name="hgrn2",
    )(hq, bf, kf, bb, kb, hi, hgate, norm_g, *consts)


RING = 4
MIXER_FFN_LAG = 2
CONF_FFN_LAG = 4
CONF_ROWS = 128


def _segment_edges(i, n_tiles, n_ctx_tiles):
    is_first = (i == 0) | (i == n_ctx_tiles)
    is_last = (i == n_tiles - 1) | (i == n_ctx_tiles - 1)
    return is_first, is_last


def _ffn_tile(x, xp, xn, mod_ref, is_first, is_last, wup_ref, dww_ref, dwb_ref, wdn_ref, lng_ref, lnb_ref):
    shift = mod_ref[0, 0, 3:4, :]
    scale = mod_ref[0, 0, 4:5, :]
    gate_mod = mod_ref[0, 0, 5:6, :]
    mod = lambda a: a * (1.0 + scale) + shift
    hp = jnp.where(is_first, 0.0, mod(xp))
    hn = jnp.where(is_last, 0.0, mod(xn))
    h_ext = jnp.concatenate([hp, mod(x), hn], axis=0).astype(BF16)
    n_ext = TM + 2 * FFN_HALO
    ug = _dot(h_ext, wup_ref[:, 0:FFN_HIDDEN])
    uv = _dot(mod(x).astype(BF16), wup_ref[:, FFN_HIDDEN:])
    lo = pltpu.roll(ug, 1, axis=0)[FFN_HALO:FFN_HALO + TM]
    hi = pltpu.roll(ug, n_ext - 1, axis=0)[FFN_HALO:FFN_HALO + TM]
    conv = (lo * dww_ref[0:1, :] + ug[FFN_HALO:FFN_HALO + TM] * dww_ref[1:2, :] + hi * dww_ref[2:3, :]
            + dwb_ref[...])
    act = (_silu(conv) * uv).astype(BF16)
    y = _dot(act, wdn_ref[...])
    out = _layer_norm(DEEPNORM_ALPHA * x + gate_mod * y, lng_ref[...], lnb_ref[...])
    marks = [ug[0:SUBLANES, c:c + LANES] for c in range(0, FFN_HIDDEN, MXU_N)]
    return out, marks


def _ring_tiles(s, ring_ref, lag):
    cur = (s + 2 * RING - lag) % RING
    prv = (s + 2 * RING - lag - 1) % RING
    nxt = (s + 2 * RING - lag + 1) % RING
    return ring_ref[cur], ring_ref[prv, TM - FFN_HALO:TM, :], ring_ref[nxt, 0:FFN_HALO, :]


def _lagged_edges(s, lag, n_tiles, n_ctx_tiles, n_total):
    t = jnp.clip(s - lag, 0, n_total - 1)
    return _segment_edges(t % n_tiles, n_tiles, n_ctx_tiles)


def _conf_glu(x, xp, xn, mod_ref, is_first, is_last, w1_ref, b1_ref):
    shift = mod_ref[0, 0, 0:1, :]
    scale = mod_ref[0, 0, 1:2, :]
    n_ext = TM + 2 * CONF_HALO
    x_ext = jnp.concatenate([xp, x, xn], axis=0)
    h_ext = (x_ext * (1.0 + scale) + shift).astype(BF16)
    a = _dot(h_ext, w1_ref[...]) + b1_ref[...]
    a = a[:, :D_MODEL] * _sigmoid(a[:, D_MODEL:])
    rows = lax.broadcasted_iota(jnp.int32, (n_ext, 1), 0)
    pad = ((rows < CONF_HALO) & is_first) | ((rows >= CONF_HALO + TM) & is_last)
    return jnp.where(pad, 0.0, a)


def _conf_dwconv(a_ref, ia, c_ref, ic, dww_ref, dwb_ref, marks, zero_ref):
    half = (CONF_KERNEL - 1) // 2
    base = CONF_HALO - half
    n_slab = CONF_ROWS + 2 * CONF_HALO

    for cb in range(D_MODEL // LANES):
        lanes = slice(cb * LANES, (cb + 1) * LANES)
        for rb in range(TM // CONF_ROWS):
            slab = a_ref[ia, rb * CONF_ROWS:rb * CONF_ROWS + n_slab, lanes]
            mark = marks[(cb * (TM // CONF_ROWS) + rb) % len(marks)]
            edge = pltpu.bitcast(pltpu.bitcast(mark, jnp.int32) & zero_ref[...], F32)
            acc = jnp.broadcast_to(dwb_ref[:, lanes] + edge[0:1, :], (CONF_ROWS, LANES))
            for r in range(SUBLANES):
                sh = pltpu.roll(slab, n_slab - r, axis=0) if r else slab
                for mm in range(-(-(base + CONF_KERNEL) // SUBLANES)):
                    kk = SUBLANES * mm + r - base
                    if 0 <= kk < CONF_KERNEL:
                        acc = acc + sh[SUBLANES * mm:SUBLANES * mm + CONF_ROWS] * dww_ref[kk:kk + 1, lanes]
            c_ref[ic, rb * CONF_ROWS:(rb + 1) * CONF_ROWS, lanes] = acc


def _conf_out(conv, x, mod_ref, cg_ref, cb_ref, w2_ref, b2_ref, lng_ref, lnb_ref):
    gate_mod = mod_ref[0, 0, 2:3, :]
    z = _silu(_layer_norm(conv, cg_ref[...], cb_ref[...])).astype(BF16)
    y = _dot(z, w2_ref[...]) + b2_ref[...]
    return _layer_norm(DEEPNORM_ALPHA * x + gate_mod * y, lng_ref[...], lnb_ref[...])


def _zero_once(s, *refs):
    @pl.when(s == 0)
    def _():
        for ref in refs:
            ref[...] = jnp.zeros(ref.shape, F32)


def _conf_ffn_kernel(x_ref, xp_ref, xn_ref, xres_ref, moda_ref, modc_ref, zero_ref, modf_ref,
                     w1_ref, b1_ref, cdww_ref, cdwb_ref, cg_ref, cb_ref, w2_ref, b2_ref, tlng_ref, tlnb_ref,
                     wup_ref, fdww_ref, fdwb_ref, wdn_ref, flng_ref, flnb_ref,
                     o_ref, ring_ref, a_ref, c_ref, *, n_tiles, n_ctx_tiles, n_total):
    s = pl.program_id(0)
    _zero_once(s, ring_ref, a_ref, c_ref)
    edges = functools.partial(_lagged_edges, s, n_tiles=n_tiles, n_ctx_tiles=n_ctx_tiles, n_total=n_total)
    ffn_in = _ring_tiles(s, ring_ref, CONF_FFN_LAG)
    ring_ref[(s + 2) % RING] = _conf_out(c_ref[s % 2], xres_ref[0], modc_ref, cg_ref, cb_ref, w2_ref, b2_ref,
                                         tlng_ref, tlnb_ref)
    o_ref[0], marks = _ffn_tile(*ffn_in, modf_ref, *edges(CONF_FFN_LAG),
                                wup_ref, fdww_ref, fdwb_ref, wdn_ref, flng_ref, flnb_ref)
    _conf_dwconv(a_ref, (s + 1) % 2, c_ref, (s + 1) % 2, cdww_ref, cdwb_ref, marks, zero_ref)
    a_ref[s % 2] = _conf_glu(x_ref[0], xp_ref[0], xn_ref[0], moda_ref, *edges(0), w1_ref, b1_ref)


def _mixout_ffn_kernel(att_ref, hg_ref, *refs, n_streams, n_tiles, n_ctx_tiles, n_total):
    x_refs, refs = refs[:n_streams], refs[n_streams:]
    (modt_ref, modf_ref, wo_ref, tlng_ref, tlnb_ref, wup_ref, fdww_ref, fdwb_ref, wdn_ref, flng_ref, flnb_ref,
     o_ref, ring_ref) = refs
    s = pl.program_id(0)
    _zero_once(s, ring_ref)
    ffn_in = _ring_tiles(s, ring_ref, MIXER_FFN_LAG)
    y = _dot(att_ref[0], wo_ref[0:ATT_Q_W, :]) + _dot(hg_ref[0], wo_ref[ATT_Q_W:, :])
    o_ref[0], _ = _ffn_tile(*ffn_in, modf_ref, *_lagged_edges(s, MIXER_FFN_LAG, n_tiles, n_ctx_tiles, n_total),
                            wup_ref, fdww_ref, fdwb_ref, wdn_ref, flng_ref, flnb_ref)
    gate = modt_ref[0, 0, 2:3, :]
    x = _stream_tile(x_refs, jnp.minimum(s, n_total - 1) % n_tiles, n_ctx_tiles)
    ring_ref[s % RING] = _layer_norm(DEEPNORM_ALPHA * x + gate * y, tlng_ref[...], tlnb_ref[...])


def _tail_ffn(kind, acts, mods, tail_params, ffn_params, n_in_ctx_tiles, skip_ctx):
    B, S, _ = acts[0].shape
    off = n_in_ctx_tiles if skip_ctx else 0
    nt = S // TM - off
    nct = 0 if skip_ctx else n_in_ctx_tiles
    n_total = B * nt
    ffn_lag = CONF_FFN_LAG if kind == "conformer" else MIXER_FFN_LAG

    def lagged(lag):
        def tile(s):
            t = jnp.clip(s - lag, 0, n_total - 1)
            return t // nt, t % nt
        return tile

    tail_tile = lagged(0)

    def rows(w, tile_fn=tail_tile):
        def index(s):
            b, i = tile_fn(s)
            return (b, i + off, 0)
        return pl.BlockSpec((1, TM, w), index)

    def halo(before):
        per = TM // CONF_HALO

        def index(s):
            b, i = tail_tile(s)
            blk = (i + off) * per - 1 if before else (i + off + 1) * per
            return (b, jnp.clip(blk, 0, (S // TM) * per - 1), 0)
        return pl.BlockSpec((1, CONF_HALO, D_MODEL), index)

    def mod_spec(tile_fn):
        def index(s):
            b, i = tile_fn(s)
            return (b, jnp.where(i < nct, 0, 1), 0, 0)
        return pl.BlockSpec((1, 1, 6, D_MODEL), index)

    def out_index(s):
        b, i = lagged(ffn_lag)(s)
        return (b, i, 0)

    small = lambda a: pl.BlockSpec(a.shape, lambda s: (0,) * a.ndim)
    spec_of = lambda a: _resident(a.shape) if a.size * a.dtype.itemsize > (1 << 20) else small(a)
    scratch = [pltpu.VMEM((RING, TM, D_MODEL), F32)]
    if kind == "conformer":
        xs, = acts
        body = _conf_ffn_kernel
        zero = jnp.zeros((SUBLANES, LANES), jnp.int32)
        in_specs = [rows(D_MODEL), halo(True), halo(False), rows(D_MODEL, lagged(2)),
                    mod_spec(tail_tile), mod_spec(lagged(2)), small(zero)]
        operands = [xs, xs, xs, xs, mods, mods, zero]
        scratch += [pltpu.VMEM((2, TM + 2 * CONF_HALO, D_MODEL), F32), pltpu.VMEM((2, TM, D_MODEL), F32)]
    else:
        streams = acts[2:]
        assert len(streams) == 1 or not skip_ctx
        body = functools.partial(_mixout_ffn_kernel, n_streams=len(streams))
        in_specs = [rows(a.shape[2]) for a in acts[:2]]
        in_specs += [rows(D_MODEL)] if len(streams) == 1 else _stream_specs(streams, nct, tail_tile)
        in_specs += [mod_spec(tail_tile)]
        operands = list(acts) + [mods]
    in_specs += [mod_spec(lagged(ffn_lag))] + [spec_of(a) for a in tail_params + ffn_params]
    operands += [mods] + list(tail_params) + list(ffn_params)
    return pl.pallas_call(
        functools.partial(body, n_tiles=nt, n_ctx_tiles=nct, n_total=n_total),
        grid=(n_total + ffn_lag,),
        in_specs=in_specs,
        out_specs=pl.BlockSpec((1, TM, D_MODEL), out_index),
        out_shape=jax.ShapeDtypeStruct((B, nt * TM, D_MODEL), F32),
        scratch_shapes=scratch,
        compiler_params=_cparams(1),
        name=kind + "_ffn",
    )(*operands)


def _rope_tables(n_ctx, n_lat):
    rows = n_lat // GRID_W
    row = jnp.repeat(jnp.arange(rows), GRID_W).astype(F32)
    col = jnp.tile(jnp.arange(GRID_W), rows).astype(F32)
    quarter = ATT_HEAD_DIM // 4
    inv_freq = ROPE_BASE ** (-jnp.arange(quarter, dtype=F32) / quarter)
    ang_r = row[:, None] * inv_freq
    ang_c = col[:, None] * inv_freq
    ang = jnp.concatenate([ang_r, ang_r, ang_c, ang_c], axis=-1)
    ang = jnp.concatenate([jnp.zeros((n_ctx, ATT_HEAD_DIM), F32), ang], axis=0)
    ang = jnp.concatenate([ang, ang], axis=-1)
    cos, sin = jnp.cos(ang), jnp.sin(ang)
    low = (jnp.arange(LANES) % (2 * quarter)) < quarter
    return cos, jnp.where(low, -sin, 0.0), jnp.where(low, 0.0, sin)


def _chunk_constants():
    t = jnp.arange(CHUNK)
    full = lambda col: jnp.broadcast_to(col[:, None], (CHUNK, HG_DIM)).astype(F32)
    sgn, mask = [], []
    for d in range(2):
        sgn_d, mask_d = [], []
        for lv in range(CHUNK_LEVELS):
            is_q = ((t >> lv) & 1) == (1 - d)
            same = (t[:, None] >> (lv + 1)) == (t[None, :] >> (lv + 1))
            sgn_d.append(full(jnp.where(is_q, 1.0, -1.0)))
            mask_d.append((same & is_q[:, None] & ~is_q[None, :]).astype(F32))
        mask_d.append((t[:, None] == t[None, :]).astype(F32))
        sgn.append(jnp.stack(sgn_d))
        mask.append(jnp.stack(mask_d))
    odd = (t & 1) == 1
    r = t & 3
    coef = jnp.stack([
        jnp.stack([full(odd), full(~odd), full(r == 3), full(r == 2), full(r == 1), full(r == 0)]),
        jnp.stack([full(~odd), full(odd), full(r == 0), full(r == 3), full(r == 2), full(r == 1)]),
    ])
    return jnp.stack(sgn), jnp.stack(mask), coef


def _prefix_matrices():
    t = jnp.arange(CHUNK)
    lower = t[None, :] <= t[:, None]
    return jnp.stack([lower, lower.T]).astype(BF16)


def kernel(x, c, ctx, c_ctx, mod_w, mod_b, post_ln_g, post_ln_b, mix_w_in, mix_w_out, att_sink, hg_lb_logits, hg_norm_g, conf_pw1_w, conf_pw1_b, conf_dw_w, conf_dw_b, conf_ln_g, conf_ln_b, conf_pw2_w, conf_pw2_b, ffn_w_up, ffn_dw_w, ffn_dw_b, ffn_w_down):
    B, T, D = x.shape
    L = ctx.shape[1]
    assert D == D_MODEL and L % TM == 0 and T % TM == 0 and L >= ATT_BLOCK and T % GRID_W == 0
    n_ctx_tiles = L // TM

    n_cond = -(-(B + 1) // SUBLANES) * SUBLANES
    cond = jnp.zeros((n_cond, D), F32).at[:B].set(c).at[B].set(c_ctx)
    mod = _modulation(cond, mod_w.astype(BF16), mod_b[:, None, :])
    mod_x = mod[:, :B].reshape(DEPTH, B, 1, 6, D)
    mod_c = jnp.broadcast_to(mod[:, B].reshape(DEPTH, 1, 1, 6, D), (DEPTH, B, 1, 6, D))
    mods = jnp.concatenate([mod_c, mod_x], axis=2)

    cos, sa, sb = _rope_tables(L, T)
    chunk_consts = _chunk_constants()
    tri = _prefix_matrices()
    vec = lambda a: a.reshape(1, -1)

    streams = (ctx, x)
    has_ctx = True
    for layer in range(DEPTH):
        m = layer // 2
        need_ctx_out = any(j % 2 == 0 for j in range(layer + 1, DEPTH))
        nct = n_ctx_tiles if has_ctx else 0
        lg, lb = post_ln_g[layer], post_ln_b[layer]
        ffn_params = (ffn_w_up[layer].astype(BF16), ffn_dw_w[layer], vec(ffn_dw_b[layer]),
                      ffn_w_down[layer].astype(BF16), vec(lg[1]), vec(lb[1]))
        if layer % 2 == 0:
            q, k, v, hq, bf, kf, bb, kb, hi, hgate = _mix_in(
                streams, mods[layer], mix_w_in[m].astype(BF16), cos, sa, sb, hg_lb_logits, tri, m, nct)
            att = _attention(att_sink[m], q, k, v, L, need_ctx_out)
            hg = _hgrn(hq, bf, kf, bb, kb, hi, hgate, vec(hg_norm_g[m]), chunk_consts, L)
            tail_params = (mix_w_out[m].astype(BF16), vec(lg[0]), vec(lb[0]))
            xs = _tail_ffn("mixer", (att, hg) + streams, mods[layer], tail_params, ffn_params, nct,
                           not need_ctx_out)
        else:
            tail_params = (conf_pw1_w[m].astype(BF16), vec(conf_pw1_b[m]), conf_dw_w[m], vec(conf_dw_b[m]),
                           vec(conf_ln_g[m]), vec(conf_ln_b[m]), conf_pw2_w[m].astype(BF16), vec(conf_pw2_b[m]),
                           vec(lg[0]), vec(lb[0]))
            xs = _tail_ffn("conformer", streams, mods[layer], tail_params, ffn_params, nct,
                           has_ctx and not need_ctx_out)
        streams = (xs,)
        has_ctx = has_ctx and need_ctx_out
    return xs[:, L:] if has_ctx else xs
```
